```python
import jax, jax.numpy as jnp
from jax import lax
import numpy as np

D_MODEL = 1024
BATCH = 8
SEQ = 2048
DEPTH = 4
DEC_BATCH = 128
DEC_SEQ = 4
PAST_LEN = 16384
PAGE_SIZE = 128

D_INNER = 2 * D_MODEL
LRU_BLOCK = 128
LRU_HEADS = D_INNER // LRU_BLOCK
LRU_CONV_W = 4
LRU_C = 8.0
CCM_CONV_W = 31
N_LRU = (DEPTH + 1) // 2
N_CCM = DEPTH // 2
RMS_EPS = 1e-6
LN_EPS = 1e-5

kernel_name = 'hawk_conformer_hybrid_step'


def rmsnorm(x, g):
    xf = x.astype(jnp.float32)
    y = xf * lax.rsqrt(jnp.mean(xf * xf, axis=-1, keepdims=True) + RMS_EPS)
    return (y * g.astype(jnp.float32)).astype(x.dtype)


def layernorm(x, g, b):
    xf = x.astype(jnp.float32)
    mu = jnp.mean(xf, axis=-1, keepdims=True)
    xc = xf - mu
    var = jnp.mean(xc * xc, axis=-1, keepdims=True)
    y = xc * lax.rsqrt(var + LN_EPS) * g.astype(jnp.float32) + b.astype(jnp.float32)
    return y.astype(x.dtype)


def causal_dwconv(x, buf, w, b):
    C = x.shape[-1]
    W = w.shape[0]
    xp = jnp.concatenate([buf.astype(x.dtype), x], axis=1)
    y = lax.conv_general_dilated(
        xp, w.astype(x.dtype)[:, None, :], window_strides=(1,), padding='VALID',
        dimension_numbers=('NWC', 'WIO', 'NWC'), feature_group_count=C)
    new_buf = xp[:, xp.shape[1] - (W - 1):]
    return y + b.astype(x.dtype), new_buf


def rg_lru(x, h0, w_a, b_a, w_i, b_i, lam):
    B, T, C = x.shape
    xf = x.astype(jnp.float32)
    xh = xf.reshape(B, T, LRU_HEADS, LRU_BLOCK)
    r = jax.nn.sigmoid(jnp.einsum('bthi,hij->bthj', xh, w_a.astype(jnp.float32)).reshape(B, T, C)
                       + b_a.astype(jnp.float32))
    i = jax.nn.sigmoid(jnp.einsum('bthi,hij->bthj', xh, w_i.astype(jnp.float32)).reshape(B, T, C)
                       + b_i.astype(jnp.float32))
    log_a = -LRU_C * r * jax.nn.softplus(-lam.astype(jnp.float32))
    a = jnp.exp(log_a)
    b = jnp.sqrt(-jnp.expm1(2.0 * log_a)) * (i * xf)
    b = b.at[:, 0].add(a[:, 0] * h0.astype(jnp.float32))

    def combine(left, right):
        a1, b1 = left
        a2, b2 = right
        return a1 * a2, a2 * b1 + b2

    _, h = lax.associative_scan(combine, (a, b), axis=1)
    return h.astype(x.dtype), h[:, -1].astype(x.dtype)


def lru_layer(x, conv_buf, h0, g, w_in, conv_w, conv_b, w_a, b_a, w_i, b_i, lam, w_out):
    hn = rmsnorm(x, g)
    u = hn @ w_in.astype(x.dtype)
    xb, z = u[..., :D_INNER], u[..., D_INNER:]
    xb, new_buf = causal_dwconv(xb, conv_buf, conv_w, conv_b)
    y, h_last = rg_lru(xb, h0, w_a, b_a, w_i, b_i, lam)
    out = (y * jax.nn.silu(z)) @ w_out.astype(x.dtype)
    return x + out, new_buf, h_last


def ccm_layer(x, conv_buf, g, w_in, b_in, dw_w, dw_b, ln_g, ln_b, w_out, b_out):
    hn = rmsnorm(x, g)
    u = hn @ w_in.astype(x.dtype) + b_in.astype(x.dtype)
    v = u[..., :D_INNER]
    gl = u[..., D_INNER:2 * D_INNER]
    z = u[..., 2 * D_INNER:]
    c = v * jax.nn.sigmoid(gl)
    c, new_buf = causal_dwconv(c, conv_buf, dw_w, dw_b)
    c = jax.nn.silu(layernorm(c, ln_g, ln_b))
    out = (c * jax.nn.silu(z)) @ w_out.astype(x.dtype) + b_out.astype(x.dtype)
    return x + out, new_buf


def setup_inputs(seed: int = 0) -> dict:
    key = jax.random.key(seed)
    ks = jax.random.split(key, 32)
    f32 = jnp.float32

    def nrm(k, shape, s):
        return jax.random.normal(k, shape, f32) * s

    u = jax.random.uniform(ks[9], (N_LRU, D_INNER), f32, minval=0.9, maxval=0.999)
    return {
        'x_prompt': nrm(ks[0], (BATCH, SEQ, D_MODEL), 1.0),
        'x_sample': nrm(ks[1], (DEC_BATCH, DEC_SEQ, D_MODEL), 1.0),
        'state_lru_conv': nrm(ks[2], (N_LRU, DEC_BATCH, LRU_CONV_W - 1, D_INNER), 1.0),
        'state_lru_h': nrm(ks[3], (N_LRU, DEC_BATCH, D_INNER), 0.5),
        'state_ccm_conv': nrm(ks[4], (N_CCM, DEC_BATCH, CCM_CONV_W - 1, D_INNER), 0.5),
        'norm_g': 1.0 + nrm(ks[5], (DEPTH, D_MODEL), 0.02),
        'final_norm_g': 1.0 + nrm(ks[6], (D_MODEL,), 0.02),
        'lru_w_in': nrm(ks[7], (N_LRU, D_MODEL, 2 * D_INNER), D_MODEL ** -0.5),
        'lru_conv_w': nrm(ks[8], (N_LRU, LRU_CONV_W, D_INNER), LRU_CONV_W ** -0.5),
        'lru_conv_b': nrm(ks[10], (N_LRU, D_INNER), 0.02),
        'lru_w_a': nrm(ks[11], (N_LRU, LRU_HEADS, LRU_BLOCK, LRU_BLOCK), LRU_BLOCK ** -0.5),
        'lru_b_a': nrm(ks[12], (N_LRU, D_INNER), 0.02),
        'lru_w_i': nrm(ks[13], (N_LRU, LRU_HEADS, LRU_BLOCK, LRU_BLOCK), LRU_BLOCK ** -0.5),
        'lru_b_i': nrm(ks[14], (N_LRU, D_INNER), 0.02),
        'lru_lam': jnp.log(u) - jnp.log1p(-u),
        'lru_w_out': nrm(ks[15], (N_LRU, D_INNER, D_MODEL), D_INNER ** -0.5),
        'ccm_w_in': nrm(ks[16], (N_CCM, D_MODEL, 3 * D_INNER), D_MODEL ** -0.5),
        'ccm_b_in': nrm(ks[17], (N_CCM, 3 * D_INNER), 0.02),
        'ccm_dw_w': nrm(ks[18], (N_CCM, CCM_CONV_W, D_INNER), CCM_CONV_W ** -0.5),
        'ccm_dw_b': nrm(ks[19], (N_CCM, D_INNER), 0.02),
        'ccm_ln_g': 1.0 + nrm(ks[20], (N_CCM, D_INNER), 0.02),
        'ccm_ln_b': nrm(ks[21], (N_CCM, D_INNER), 0.02),
        'ccm_w_out': nrm(ks[22], (N_CCM, D_INNER, D_MODEL), D_INNER ** -0.5),
        'ccm_b_out': nrm(ks[23], (N_CCM, D_MODEL), 0.02),
    }


def reference(x_prompt, x_sample, state_lru_conv, state_lru_h, state_ccm_conv,
              norm_g, final_norm_g,
              lru_w_in, lru_conv_w, lru_conv_b, lru_w_a, lru_b_a, lru_w_i, lru_b_i, lru_lam, lru_w_out,
              ccm_w_in, ccm_b_in, ccm_dw_w, ccm_dw_b, ccm_ln_g, ccm_ln_b, ccm_w_out, ccm_b_out):
    xp, xs = x_prompt, x_sample
    Bp = x_prompt.shape[0]
    dt = x_prompt.dtype
    lru_conv_p, lru_h_p, ccm_conv_p = [], [], []
    lru_conv_s, lru_h_s, ccm_conv_s = [], [], []
    for l in range(DEPTH):
        j = l // 2
        if l % 2 == 0:
            lw = (norm_g[l], lru_w_in[j], lru_conv_w[j], lru_conv_b[j], lru_w_a[j], lru_b_a[j],
                  lru_w_i[j], lru_b_i[j], lru_lam[j], lru_w_out[j])
            zc = jnp.zeros((Bp, LRU_CONV_W - 1, D_INNER), dt)
            zh = jnp.zeros((Bp, D_INNER), dt)
            xp, cb, hl = lru_layer(xp, zc, zh, *lw)
            lru_conv_p.append(cb)
            lru_h_p.append(hl)
            xs, cb, hl = lru_layer(xs, state_lru_conv[j], state_lru_h[j], *lw)
            lru_conv_s.append(cb)
            lru_h_s.append(hl)
        else:
            cw = (norm_g[l], ccm_w_in[j], ccm_b_in[j], ccm_dw_w[j], ccm_dw_b[j],
                  ccm_ln_g[j], ccm_ln_b[j], ccm_w_out[j], ccm_b_out[j])
            zc = jnp.zeros((Bp, CCM_CONV_W - 1, D_INNER), dt)
            xp, cb = ccm_layer(xp, zc, *cw)
            ccm_conv_p.append(cb)
            xs, cb = ccm_layer(xs, state_ccm_conv[j], *cw)
            ccm_conv_s.append(cb)
    y_prompt = rmsnorm(xp, final_norm_g)
    y_sample = rmsnorm(xs, final_norm_g)
    return (y_prompt, y_sample,
            jnp.stack(lru_conv_p), jnp.stack(lru_h_p), jnp.stack(ccm_conv_p),
            jnp.stack(lru_conv_s), jnp.stack(lru_h_s), jnp.stack(ccm_conv_s))
```

```python
import functools

import jax
import jax.numpy as jnp
from jax import lax
from jax.experimental import pallas as pl
from jax.experimental.pallas import tpu as pltpu

F32 = jnp.float32
BF16 = jnp.bfloat16

RMS_EPS = 1e-6
LN_EPS = 1e-5
LRU_C = 8.0
LRU_BLOCK = 128
LRU_CONV_W = 4
CCM_CONV_W = 31

V7X_VMEM_BYTES = 64 * 1024 * 1024
SUBLANES = 8
LANES = 128
VMEM_LIMIT_BYTES = V7X_VMEM_BYTES - 8 * 1024 * 1024

CONV_ROW_TILE = 32
CONV_COL_TILE = 512


def _rmsnorm(x, g):
    ms = jnp.mean(x * x, axis=-1, keepdims=True)
    return (x * lax.rsqrt(ms + RMS_EPS)) * g


def _silu(x):
    return x * jax.nn.sigmoid(x)


def _lru_kernel(x_ref, g_ref, win_ref, cw_ref, cb_ref, wg_ref, ba_ref, bi_ref,
                lam_ref, wout_ref, conv0_ref, h0_ref,
                xo_ref, convo_ref, ho_ref,
                xp_s, xc_s, z_s, a_s, b_s, h_s, *, tt, bt, nt):
    d = x_ref.shape[-1]
    c = xp_s.shape[-1]
    r = tt * bt
    p = (LRU_CONV_W - 1) * bt
    it = pl.program_id(1)

    @pl.when(it == 0)
    def _():
        xp_s[0:p, :] = conv0_ref[...].reshape(p, c)
        h_s[...] = h0_ref[...]

    x = x_ref[...].reshape(r, d)
    hn = _rmsnorm(x, g_ref[...]).astype(BF16)
    xp_s[p:p + r, :] = jnp.dot(hn, win_ref[:, 0:c], preferred_element_type=F32)
    z_s[...] = jnp.dot(hn, win_ref[:, c:2 * c], preferred_element_type=F32)

    acc = cb_ref[...] + cw_ref[0:1, :] * xp_s[0:r, :]
    for k in range(1, LRU_CONV_W):
        acc = acc + cw_ref[k:k + 1, :] * xp_s[k * bt:k * bt + r, :]
    xc_s[...] = acc

    sp = jax.nn.softplus(-lam_ref[...])
    for h in range(c // LRU_BLOCK):
        cols = slice(h * LRU_BLOCK, (h + 1) * LRU_BLOCK)
        xc = xc_s[:, cols]
        gates = jnp.dot(xc.astype(BF16), wg_ref[h], preferred_element_type=F32)
        rg = jax.nn.sigmoid(gates[:, 0:LRU_BLOCK] + ba_ref[:, cols])
        ig = jax.nn.sigmoid(gates[:, LRU_BLOCK:] + bi_ref[:, cols])
        log_a = (-LRU_C * rg) * sp[:, cols]
        a = jnp.exp(log_a)
        a_s[:, cols] = a
        b_s[:, cols] = jnp.sqrt(-jnp.tanh(log_a) * (a * a + 1.0)) * (ig * xc)

    def step(t, hprev):
        rows = pl.ds(pl.multiple_of(t * bt, bt), bt)
        hcur = a_s[rows, :] * hprev + b_s[rows, :]
        b_s[rows, :] = hcur
        return hcur

    h_last = lax.fori_loop(0, tt, step, h_s[...], unroll=4)
    h_s[...] = h_last

    y = (b_s[...] * _silu(z_s[...])).astype(BF16)
    out = jnp.dot(y, wout_ref[...], preferred_element_type=F32)
    xo_ref[...] = (x + out).reshape(tt, bt, d)

    if nt > 1:
        xp_s[0:p, :] = xp_s[r:r + p, :]

    @pl.when(it == nt - 1)
    def _():
        convo_ref[...] = xp_s[r:r + p, :].reshape(LRU_CONV_W - 1, bt, c)
        ho_ref[...] = h_last


def _ccm_kernel(x_ref, g_ref, win_ref, bin_ref, dw_ref, dwb_ref, lng_ref, lnb_ref,
                wout_ref, bout_ref, fg_ref, conv0_ref,
                xo_ref, convo_ref,
                cp_s, z_s, y_s, wb_s, *, tt, bt, nt, final_norm):
    d = x_ref.shape[-1]
    c = cp_s.shape[-1]
    r = tt * bt
    p = (CCM_CONV_W - 1) * bt
    it = pl.program_id(1)

    @pl.when(it == 0)
    def _():
        cp_s[0:p, :] = conv0_ref[...].reshape(p, c)
        for k in range(CCM_CONV_W):
            wb_s[k] = jnp.broadcast_to(dw_ref[k:k + 1, :], (SUBLANES, c))

    x = x_ref[...].reshape(r, d)
    hn = _rmsnorm(x, g_ref[...]).astype(BF16)
    v = jnp.dot(hn, win_ref[:, 0:c], preferred_element_type=F32) + bin_ref[:, 0:c]
    gl = jnp.dot(hn, win_ref[:, c:2 * c], preferred_element_type=F32) + bin_ref[:, c:2 * c]
    cp_s[p:p + r, :] = v * jax.nn.sigmoid(gl)
    z_s[...] = jnp.dot(hn, win_ref[:, 2 * c:3 * c], preferred_element_type=F32) + bin_ref[:, 2 * c:3 * c]

    rt = CONV_ROW_TILE
    ct = CONV_COL_TILE
    for jc in range(c // ct):
        cols = slice(jc * ct, (jc + 1) * ct)
        bias = jnp.broadcast_to(dwb_ref[:, cols], (rt, ct))

        def row_tile(i, carry, cols=cols, bias=bias):
            r0 = pl.multiple_of(i * rt, rt)
            acc = bias
            for k in range(CCM_CONV_W):
                xk = cp_s[pl.ds(r0 + k * bt, rt), cols]
                wk = wb_s[k, :, cols]
                acc = acc + (xk.reshape(rt // SUBLANES, SUBLANES, ct) * wk[None]).reshape(rt, ct)
            y_s[pl.ds(r0, rt), cols] = acc
            return carry

        lax.fori_loop(0, r // rt, row_tile, 0)

    y = y_s[...]
    mu = jnp.mean(y, axis=-1, keepdims=True)
    yc = y - mu
    var = jnp.mean(yc * yc, axis=-1, keepdims=True)
    yn = yc * lax.rsqrt(var + LN_EPS) * lng_ref[...] + lnb_ref[...]
    s = (_silu(yn) * _silu(z_s[...])).astype(BF16)
    out = jnp.dot(s, wout_ref[...], preferred_element_type=F32) + bout_ref[...]
    xn = x + out
    if final_norm:
        xn = _rmsnorm(xn, fg_ref[...])
    xo_ref[...] = xn.reshape(tt, bt, d)

    @pl.when(it == nt - 1)
    def _():
        convo_ref[...] = cp_s[r:r + p, :].reshape(CCM_CONV_W - 1, bt, c)

    if nt > 1:
        cp_s[0:p, :] = cp_s[r:r + p, :]


def _const_spec(shape):
    nd = len(shape)
    return pl.BlockSpec(shape, lambda ib, it, nd=nd: (0,) * nd, pipeline_mode=pl.Buffered(1))


def _compiler_params():
    return pltpu.CompilerParams(
        dimension_semantics=("arbitrary", "arbitrary"),
        vmem_limit_bytes=VMEM_LIMIT_BYTES)


def _lru_layer(x, conv0, h0, g, w_in, conv_w, conv_b, w_gate, b_a, b_i, lam, w_out, *, tt, bt):
    t, b, d = x.shape
    c = w_out.shape[0]
    nt = t // tt
    nb = b // bt
    r = tt * bt
    p = (LRU_CONV_W - 1) * bt
    assert t % tt == 0 and b % bt == 0 and bt % SUBLANES == 0
    assert nt == 1 or r >= p
    kern = functools.partial(_lru_kernel, tt=tt, bt=bt, nt=nt)
    row = lambda a: a.reshape(1, -1)
    return pl.pallas_call(
        kern,
        grid=(nb, nt),
        in_specs=[
            pl.BlockSpec((tt, bt, d), lambda ib, it: (it, ib, 0)),
            _const_spec((1, d)),
            _const_spec((d, 2 * c)),
            _const_spec((LRU_CONV_W, c)),
            _const_spec((1, c)),
            _const_spec(w_gate.shape),
            _const_spec((1, c)),
            _const_spec((1, c)),
            _const_spec((1, c)),
            _const_spec((c, d)),
            pl.BlockSpec((LRU_CONV_W - 1, bt, c), lambda ib, it: (0, ib, 0)),
            pl.BlockSpec((bt, c), lambda ib, it: (ib, 0)),
        ],
        out_specs=[
            pl.BlockSpec((tt, bt, d), lambda ib, it: (it, ib, 0)),
            pl.BlockSpec((LRU_CONV_W - 1, bt, c), lambda ib, it: (0, ib, 0)),
            pl.BlockSpec((bt, c), lambda ib, it: (ib, 0)),
        ],
        out_shape=[
            jax.ShapeDtypeStruct((t, b, d), F32),
            jax.ShapeDtypeStruct((LRU_CONV_W - 1, b, c), F32),
            jax.ShapeDtypeStruct((b, c), F32),
        ],
        scratch_shapes=[
            pltpu.VMEM((p + r, c), F32),
            pltpu.VMEM((r, c), F32),
            pltpu.VMEM((r, c), F32),
            pltpu.VMEM((r, c), F32),
            pltpu.VMEM((r, c), F32),
            pltpu.VMEM((bt, c), F32),
        ],
        compiler_params=_compiler_params(),
        name="lru_layer",
    )(x, row(g), w_in, conv_w, row(conv_b), w_gate, row(b_a), row(b_i), row(lam), w_out, conv0, h0)


def _ccm_layer(x, conv0, g, w_in, b_in, dw_w, dw_b, ln_g, ln_b, w_out, b_out, final_g, *,
               tt, bt, final_norm):
    t, b, d = x.shape
    c = w_out.shape[0]
    nt = t // tt
    nb = b // bt
    r = tt * bt
    p = (CCM_CONV_W - 1) * bt
    assert t % tt == 0 and b % bt == 0 and bt % SUBLANES == 0
    assert r % CONV_ROW_TILE == 0 and c % CONV_COL_TILE == 0
    assert nt == 1 or r >= p
    kern = functools.partial(_ccm_kernel, tt=tt, bt=bt, nt=nt, final_norm=final_norm)
    row = lambda a: a.reshape(1, -1)
    return pl.pallas_call(
        kern,
        grid=(nb, nt),
        in_specs=[
            pl.BlockSpec((tt, bt, d), lambda ib, it: (it, ib, 0)),
            _const_spec((1, d)),
            _const_spec((d, 3 * c)),
            _const_spec((1, 3 * c)),
            _const_spec((CCM_CONV_W, c)),
            _const_spec((1, c)),
            _const_spec((1, c)),
            _const_spec((1, c)),
            _const_spec((c, d)),
            _const_spec((1, d)),
            _const_spec((1, d)),
            pl.BlockSpec((CCM_CONV_W - 1, bt, c), lambda ib, it: (0, ib, 0)),
        ],
        out_specs=[
            pl.BlockSpec((tt, bt, d), lambda ib, it: (it, ib, 0)),
            pl.BlockSpec((CCM_CONV_W - 1, bt, c), lambda ib, it: (0, ib, 0)),
        ],
        out_shape=[
            jax.ShapeDtypeStruct((t, b, d), F32),
            jax.ShapeDtypeStruct((CCM_CONV_W - 1, b, c), F32),
        ],
        scratch_shapes=[
            pltpu.VMEM((p + r, c), F32),
            pltpu.VMEM((r, c), F32),
            pltpu.VMEM((r, c), F32),
            pltpu.VMEM((CCM_CONV_W, SUBLANES, c), F32),
        ],
        compiler_params=_compiler_params(),
        name="ccm_layer",
    )(x, row(g), w_in, row(b_in), dw_w, row(dw_b), row(ln_g), row(ln_b), w_out, row(b_out),
      row(final_g), conv0)


def kernel(x_prompt, x_sample, state_lru_conv, state_lru_h, state_ccm_conv, norm_g, final_norm_g,
           lru_w_in, lru_conv_w, lru_conv_b, lru_w_a, lru_b_a, lru_w_i, lru_b_i, lru_lam, lru_w_out,
           ccm_w_in, ccm_b_in, ccm_dw_w, ccm_dw_b, ccm_ln_g, ccm_ln_b, ccm_w_out, ccm_b_out):
    depth = norm_g.shape[0]
    bp = x_prompt.shape[0]
    c = lru_w_out.shape[1]
    tm = lambda a: jnp.swapaxes(a, -3, -2)

    xp = tm(x_prompt)
    xs = tm(x_sample)
    s_lru_conv = tm(state_lru_conv)
    s_ccm_conv = tm(state_ccm_conv)
    z_lru_conv = jnp.zeros((LRU_CONV_W - 1, bp, c), F32)
    z_lru_h = jnp.zeros((bp, c), F32)
    z_ccm_conv = jnp.zeros((CCM_CONV_W - 1, bp, c), F32)

    lru_w_in_b = lru_w_in.astype(BF16)
    lru_w_out_b = lru_w_out.astype(BF16)
    lru_w_gate_b = jnp.concatenate([lru_w_a, lru_w_i], axis=-1).astype(BF16)
    ccm_w_in_b = ccm_w_in.astype(BF16)
    ccm_w_out_b = ccm_w_out.astype(BF16)

    lru_conv_p, lru_h_p, ccm_conv_p = [], [], []
    lru_conv_s, lru_h_s, ccm_conv_s = [], [], []
    for l in range(depth):
        j = l // 2
        if l % 2 == 0:
            lw = (norm_g[l], lru_w_in_b[j], lru_conv_w[j], lru_conv_b[j], lru_w_gate_b[j],
                  lru_b_a[j], lru_b_i[j], lru_lam[j], lru_w_out_b[j])
            xp, cb, hl = _lru_layer(xp, z_lru_conv, z_lru_h, *lw, tt=32, bt=bp)
            lru_conv_p.append(cb)
            lru_h_p.append(hl)
            xs, cb, hl = _lru_layer(xs, s_lru_conv[j], state_lru_h[j], *lw, tt=xs.shape[0], bt=64)
            lru_conv_s.append(cb)
            lru_h_s.append(hl)
        else:
            last = l == depth - 1
            cw = (norm_g[l], ccm_w_in_b[j], ccm_b_in[j], ccm_dw_w[j], ccm_dw_b[j], ccm_ln_g[j],
                  ccm_ln_b[j], ccm_w_out_b[j], ccm_b_out[j], final_norm_g)
            xp, cb = _ccm_layer(xp, z_ccm_conv, *cw, tt=32, bt=bp, final_norm=last)
            ccm_conv_p.append(cb)
            xs, cb = _ccm_layer(xs, s_ccm_conv[j], *cw, tt=xs.shape[0], bt=16, final_norm=last)
            ccm_conv_s.append(cb)
    assert depth % 2 == 0
    return (tm(xp), tm(xs),
            tm(jnp.stack(lru_conv_p)), jnp.stack(lru_h_p), tm(jnp.stack(ccm_conv_p)),
            tm(jnp.stack(lru_conv_s)), jnp.stack(lru_h_s), tm(jnp.stack(ccm_conv_s)))
```

```python
import functools

import jax
import jax.numpy as jnp
from jax import lax
from jax.experimental import pallas as pl
from jax.experimental.pallas import tpu as pltpu

F32 = jnp.float32
BF16 = jnp.bfloat16

RMS_EPS = 1e-6
LN_EPS = 1e-5
LRU_C = 8.0
LRU_BLOCK = 128
LRU_CONV_W = 4
CCM_CONV_W = 31

V7X_VMEM_BYTES = 64 * 1024 * 1024
SUBLANES = 8
LANES = 128
VMEM_LIMIT_BYTES = V7X_VMEM_BYTES - 8 * 1024 * 1024

NORM_ROW_TILE = 32
CONV_ACC_VREGS = 8
MXU_COLS = 256
MIN_OUT_ROWS = 128


def _rmsnorm(x, g):
    ms = jnp.mean(x * x, axis=-1, keepdims=True)
    return (x * lax.rsqrt(ms + RMS_EPS)) * g


def _silu(x):
    return x * jax.nn.sigmoid(x)


def _pack_rows(w):
    *lead, k, n = w.shape
    pairs = jnp.swapaxes(w.astype(BF16).reshape(*lead, k // 2, 2, n), -1, -2)
    return lax.bitcast_convert_type(pairs, jnp.uint32)


def _rows_bf16(packed):
    return pltpu.bitcast(packed, BF16)


def _interleave(front, back):
    merged, i = [], 0
    for j, item in enumerate(back):
        while i < len(front) and i * len(back) <= j * len(front):
            merged.append(front[i])
            i += 1
        merged.append(item)
    merged.extend(front[i:])
    return merged


def _run_halves(front, back, pipelined):
    for part in (_interleave(front, back) if pipelined else front + back):
        part()


def _lru_kernel(x_ref, g_ref, win_ref, cw_ref, cb_ref, wg_ref, ba_ref, bi_ref,
                lam_ref, wout_ref, conv0_ref, h0_ref,
                xo_ref, convo_ref, ho_ref,
                xp_s, z_s, xr_s, hn_s, y_s, acc_s, h_s, *, tt, bt, nt):
    d = x_ref.shape[-1]
    c = xp_s.shape[-1]
    r = tt * bt
    p = (LRU_CONV_W - 1) * bt
    pipelined = nt > 1
    nsteps = nt + 1 if pipelined else nt
    xb_off, xa_off = (p, p + r) if pipelined else (p, p)
    zb_off, za_off = (0, r) if pipelined else (0, 0)
    it = pl.program_id(1)

    @pl.when(it == 0)
    def _():
        if pipelined:
            xp_s[0:p + r, :] = jnp.zeros((p + r, c), F32)
            z_s[0:r, :] = jnp.zeros((r, c), F32)
            xr_s[0:r, :] = jnp.zeros((r, d), F32)
            h_s[...] = jnp.zeros(h_s.shape, F32)
        else:
            xp_s[0:p, :] = conv0_ref[...].reshape(p, c)
            h_s[...] = h0_ref[...]

    def front_norm():
        x = x_ref[...].reshape(r, d)
        xr_s[za_off:za_off + r, :] = x
        hn_s[...] = _rmsnorm(x, g_ref[...]).astype(BF16)

    def front_x(j):
        cols = slice(j * MXU_COLS, (j + 1) * MXU_COLS)
        xp_s[xa_off:xa_off + r, cols] = jnp.dot(hn_s[...], _rows_bf16(win_ref[:, cols]), preferred_element_type=F32)

    def front_z(j):
        cols = slice(j * MXU_COLS, (j + 1) * MXU_COLS)
        wcols = slice(c + j * MXU_COLS, c + (j + 1) * MXU_COLS)
        z_s[za_off:za_off + r, cols] = _silu(jnp.dot(hn_s[...], _rows_bf16(win_ref[:, wcols]), preferred_element_type=F32))

    def back_head(h):
        cols = slice(h * LRU_BLOCK, (h + 1) * LRU_BLOCK)
        base = xb_off - p
        xc = cb_ref[:, cols] + cw_ref[0:1, cols] * xp_s[base:base + r, cols]
        for k in range(1, LRU_CONV_W):
            xc = xc + cw_ref[k:k + 1, cols] * xp_s[base + k * bt:base + k * bt + r, cols]
        gates = jnp.dot(xc.astype(BF16), _rows_bf16(wg_ref[h]), preferred_element_type=F32)
        rg = jax.nn.sigmoid(gates[:, 0:LRU_BLOCK] + ba_ref[:, cols])
        ig = jax.nn.sigmoid(gates[:, LRU_BLOCK:] + bi_ref[:, cols])
        log_a = (-LRU_C * rg) * jax.nn.softplus(-lam_ref[:, cols])
        a = jnp.exp(log_a)
        b = jnp.sqrt(-jnp.tanh(log_a) * (a * a + 1.0)) * (ig * xc)
        hcur = h_s[:, cols]
        hs = []
        for t in range(tt):
            hcur = a[t * bt:(t + 1) * bt] * hcur + b[t * bt:(t + 1) * bt]
            hs.append(hcur)
        h_s[:, cols] = hcur
        hall = jnp.concatenate(hs, axis=0)
        y_s[:, cols] = (hall * z_s[zb_off:zb_off + r, cols]).astype(BF16)

    def back_out(j):
        rows = slice(j * MXU_COLS, (j + 1) * MXU_COLS)
        wrows = slice(j * MXU_COLS // 2, (j + 1) * MXU_COLS // 2)
        part = jnp.dot(y_s[:, rows], _rows_bf16(wout_ref[wrows, :]), preferred_element_type=F32)
        if j == 0:
            acc_s[...] = xr_s[zb_off:zb_off + r, :] + part
        elif j < c // MXU_COLS - 1:
            acc_s[...] = acc_s[...] + part
        else:
            xo_ref[...] = (acc_s[...] + part).reshape(tt, bt, d)

    heads_per_chunk = MXU_COLS // LRU_BLOCK
    front = [front_norm]
    for j in range(c // MXU_COLS):
        front += [functools.partial(front_x, j), functools.partial(front_z, j)]
    back = []
    for j in range(c // MXU_COLS):
        back += [functools.partial(back_head, j * heads_per_chunk + i) for i in range(heads_per_chunk)]
        back.append(functools.partial(back_out, j))
    _run_halves(front, back, pipelined)

    if pipelined:
        xp_s[0:p + r, :] = xp_s[r:p + 2 * r, :]
        z_s[0:r, :] = z_s[r:2 * r, :]
        xr_s[0:r, :] = xr_s[r:2 * r, :]

        @pl.when(it == 0)
        def _():
            xp_s[0:p, :] = conv0_ref[...].reshape(p, c)
            h_s[...] = h0_ref[...]

    @pl.when(it == nsteps - 1)
    def _():
        tail = 0 if pipelined else r
        convo_ref[...] = xp_s[tail:tail + p, :].reshape(LRU_CONV_W - 1, bt, c)
        ho_ref[...] = h_s[...]


def _ccm_kernel(x_ref, g_ref, win_ref, bin_ref, dw_ref, dwb_ref, lng_ref, lnb_ref,
                wout_ref, bout_ref, fg_ref, conv0_ref,
                xo_ref, convo_ref,
                cp_s, z_s, xr_s, hn_s, y_s, s_s, wb_s, *next_s, tt, bt, nt, final_norm):
    d = x_ref.shape[-1]
    c = cp_s.shape[-1]
    r = tt * bt
    p = (CCM_CONV_W - 1) * bt
    pipelined = nt > 1
    nsteps = nt + 1 if pipelined else nt
    cpa_s, cpa_off = (next_s[0], 0) if pipelined else (cp_s, p)
    za_s = next_s[1] if pipelined else z_s
    xa_off = r if pipelined else 0
    it = pl.program_id(1)

    @pl.when(it == 0)
    def _():
        for k in range(CCM_CONV_W):
            wb_s[k] = jnp.broadcast_to(dw_ref[k:k + 1, :], (SUBLANES, c))
        if pipelined:
            cp_s[0:p + r, :] = jnp.zeros((p + r, c), F32)
            z_s[0:r, :] = jnp.zeros((r, c), F32)
            xr_s[0:r, :] = jnp.zeros((r, d), F32)
        else:
            cp_s[0:p, :] = conv0_ref[...].reshape(p, c)

    def front_norm():
        x = x_ref[...].reshape(r, d)
        xr_s[xa_off:xa_off + r, :] = x
        hn_s[...] = _rmsnorm(x, g_ref[...]).astype(BF16)

    def lane_chunk(start, width):
        return pl.ds(pl.multiple_of(start, width), width)

    def front_chunk(j):
        def in_proj(part):
            wcols = lane_chunk(part * c + j * MXU_COLS, MXU_COLS)
            return jnp.dot(hn_s[...], _rows_bf16(win_ref[:, wcols]), preferred_element_type=F32) + bin_ref[:, wcols]

        cols = lane_chunk(j * MXU_COLS, MXU_COLS)
        cpa_s[cpa_off:cpa_off + r, cols] = in_proj(0) * jax.nn.sigmoid(in_proj(1))
        za_s[:, cols] = _silu(in_proj(2))

    rt = NORM_ROW_TILE
    tg = min(tt, CONV_ACC_VREGS * SUBLANES // bt)

    def back_conv(jc):
        cols = lane_chunk(jc * LANES, LANES)
        w = [wb_s[k, :, cols] for k in range(CCM_CONV_W)]
        bias = jnp.broadcast_to(dwb_ref[:, cols], (bt, LANES))
        for g in range(tt // tg):
            accs = [bias] * tg
            row0 = g * tg * bt
            for s in range(tg + CCM_CONV_W - 1):
                xs = cp_s[row0 + s * bt:row0 + (s + 1) * bt, cols]
                xs = xs.reshape(bt // SUBLANES, SUBLANES, LANES)
                for t in range(max(0, s - (CCM_CONV_W - 1)), min(tg, s + 1)):
                    accs[t] = accs[t] + (xs * w[s - t][None]).reshape(bt, LANES)
            for t in range(tg):
                y_s[(g * tg + t) * bt:(g * tg + t + 1) * bt, cols] = accs[t]

    def back_norm(i):
        r0 = i * rt
        y = y_s[r0:r0 + rt, :]
        mu = jnp.mean(y, axis=-1, keepdims=True)
        yc = y - mu
        var = jnp.mean(yc * yc, axis=-1, keepdims=True)
        yn = yc * lax.rsqrt(var + LN_EPS) * lng_ref[...] + lnb_ref[...]
        s_s[r0:r0 + rt, :] = (_silu(yn) * z_s[r0:r0 + rt, :]).astype(BF16)

    def back_out(r0, nrows):
        out = jnp.dot(s_s[r0:r0 + nrows, :], _rows_bf16(wout_ref[...]), preferred_element_type=F32) + bout_ref[...]
        xn = xr_s[r0:r0 + nrows, :] + out
        if final_norm:
            xn = _rmsnorm(xn, fg_ref[...])
        xo_ref[r0 // bt:(r0 + nrows) // bt] = xn.reshape(nrows // bt, bt, d)

    n_front = c // MXU_COLS
    conv_per_front = MXU_COLS // LANES

    def conv_chunks(j):
        for i in range(conv_per_front):
            back_conv(j * conv_per_front + i)

    front_norm()
    if pipelined:
        def paired(j, carry):
            conv_chunks(j)
            front_chunk(j)
            return carry

        lax.fori_loop(0, n_front, paired, 0)
    else:
        lax.fori_loop(0, n_front, lambda j, carry: (front_chunk(j), carry)[1], 0)
        lax.fori_loop(0, n_front, lambda j, carry: (conv_chunks(j), carry)[1], 0)
    out_rows = min(r, max(r // 2, MIN_OUT_ROWS))
    for i in range(r // rt):
        back_norm(i)
        if ((i + 1) * rt) % out_rows == 0:
            back_out((i + 1) * rt - out_rows, out_rows)

    if pipelined:
        cp_s[0:p, :] = cp_s[r:r + p, :]
        cp_s[p:p + r, :] = cpa_s[...]
        z_s[...] = za_s[...]
        xr_s[0:r, :] = xr_s[r:2 * r, :]

        @pl.when(it == 0)
        def _():
            cp_s[0:p, :] = conv0_ref[...].reshape(p, c)

    @pl.when(it == nsteps - 1)
    def _():
        tail = 0 if pipelined else r
        convo_ref[...] = cp_s[tail:tail + p, :].reshape(CCM_CONV_W - 1, bt, c)


def _const_spec(shape):
    nd = len(shape)
    return pl.BlockSpec(shape, lambda ib, it, nd=nd: (0,) * nd, pipeline_mode=pl.Buffered(1))


def _stream_specs(tt, bt, d, nt):
    if nt > 1:
        in_map = lambda ib, it: (jnp.minimum(it, nt - 1), ib, 0)
        out_map = lambda ib, it: (jnp.maximum(it - 1, 0), ib, 0)
        nsteps = nt + 1
    else:
        in_map = out_map = lambda ib, it: (it, ib, 0)
        nsteps = nt
    return pl.BlockSpec((tt, bt, d), in_map), pl.BlockSpec((tt, bt, d), out_map), nsteps


def _compiler_params():
    return pltpu.CompilerParams(
        dimension_semantics=("arbitrary", "arbitrary"),
        vmem_limit_bytes=VMEM_LIMIT_BYTES)


def _lru_layer(x, conv0, h0, g, w_in, conv_w, conv_b, w_gate, b_a, b_i, lam, w_out, *, tt, bt):
    t, b, d = x.shape
    c = conv_w.shape[-1]
    assert w_in.shape == (d // 2, 2 * c) and w_out.shape == (c // 2, d)
    nt = t // tt
    nb = b // bt
    r = tt * bt
    p = (LRU_CONV_W - 1) * bt
    assert t % tt == 0 and b % bt == 0 and bt % SUBLANES == 0 and r % (2 * SUBLANES) == 0
    assert nt == 1 or r >= p
    assert c % MXU_COLS == 0 and c // MXU_COLS >= 2 and MXU_COLS % LRU_BLOCK == 0
    nblk = 2 if nt > 1 else 1
    x_in, x_out, nsteps = _stream_specs(tt, bt, d, nt)
    kern = functools.partial(_lru_kernel, tt=tt, bt=bt, nt=nt)
    row = lambda a: a.reshape(1, -1)
    return pl.pallas_call(
        kern,
        grid=(nb, nsteps),
        in_specs=[
            x_in,
            _const_spec((1, d)),
            _const_spec((d // 2, 2 * c)),
            _const_spec((LRU_CONV_W, c)),
            _const_spec((1, c)),
            _const_spec(w_gate.shape),
            _const_spec((1, c)),
            _const_spec((1, c)),
            _const_spec((1, c)),
            _const_spec((c // 2, d)),
            pl.BlockSpec((LRU_CONV_W - 1, bt, c), lambda ib, it: (0, ib, 0)),
            pl.BlockSpec((bt, c), lambda ib, it: (ib, 0)),
        ],
        out_specs=[
            x_out,
            pl.BlockSpec((LRU_CONV_W - 1, bt, c), lambda ib, it: (0, ib, 0)),
            pl.BlockSpec((bt, c), lambda ib, it: (ib, 0)),
        ],
        out_shape=[
            jax.ShapeDtypeStruct((t, b, d), F32),
            jax.ShapeDtypeStruct((LRU_CONV_W - 1, b, c), F32),
            jax.ShapeDtypeStruct((b, c), F32),
        ],
        scratch_shapes=[
            pltpu.VMEM((p + nblk * r, c), F32),
            pltpu.VMEM((nblk * r, c), F32),
            pltpu.VMEM((nblk * r, d), F32),
            pltpu.VMEM((r, d), BF16),
            pltpu.VMEM((r, c), BF16),
            pltpu.VMEM((r, d), F32),
            pltpu.VMEM((bt, c), F32),
        ],
        compiler_params=_compiler_params(),
        name="lru_layer",
    )(x, row(g), w_in, conv_w, row(conv_b), w_gate, row(b_a), row(b_i), row(lam), w_out, conv0, h0)


def _ccm_layer(x, conv0, g, w_in, b_in, dw_w, dw_b, ln_g, ln_b, w_out, b_out, final_g, *,
               tt, bt, final_norm):
    t, b, d = x.shape
    c = dw_w.shape[-1]
    assert w_in.shape == (d // 2, 3 * c) and w_out.shape == (c // 2, d)
    nt = t // tt
    nb = b // bt
    r = tt * bt
    p = (CCM_CONV_W - 1) * bt
    assert t % tt == 0 and b % bt == 0 and bt % SUBLANES == 0
    tg = min(tt, CONV_ACC_VREGS * SUBLANES // bt)
    assert tt % tg == 0 and r % NORM_ROW_TILE == 0 and c % MXU_COLS == 0
    assert nt == 1 or r >= p
    assert r % min(r, max(r // 2, MIN_OUT_ROWS)) == 0
    nblk = 2 if nt > 1 else 1
    x_in, x_out, nsteps = _stream_specs(tt, bt, d, nt)
    kern = functools.partial(_ccm_kernel, tt=tt, bt=bt, nt=nt, final_norm=final_norm)
    row = lambda a: a.reshape(1, -1)
    return pl.pallas_call(
        kern,
        grid=(nb, nsteps),
        in_specs=[
            x_in,
            _const_spec((1, d)),
            _const_spec((d // 2, 3 * c)),
            _const_spec((1, 3 * c)),
            _const_spec((CCM_CONV_W, c)),
            _const_spec((1, c)),
            _const_spec((1, c)),
            _const_spec((1, c)),
            _const_spec((c // 2, d)),
            _const_spec((1, d)),
            _const_spec((1, d)),
            pl.BlockSpec((CCM_CONV_W - 1, bt, c), lambda ib, it: (0, ib, 0)),
        ],
        out_specs=[
            x_out,
            pl.BlockSpec((CCM_CONV_W - 1, bt, c), lambda ib, it: (0, ib, 0)),
        ],
        out_shape=[
            jax.ShapeDtypeStruct((t, b, d), F32),
            jax.ShapeDtypeStruct((CCM_CONV_W - 1, b, c), F32),
        ],
        scratch_shapes=[
            pltpu.VMEM((p + r, c), F32),
            pltpu.VMEM((r, c), F32),
            pltpu.VMEM((nblk * r, d), F32),
            pltpu.VMEM((r, d), BF16),
            pltpu.VMEM((r, c), F32),
            pltpu.VMEM((r, c), BF16),
            pltpu.VMEM((CCM_CONV_W, SUBLANES, c), F32),
        ] + [pltpu.VMEM((r, c), F32)] * (2 * (nblk - 1)),
        compiler_params=_compiler_params(),
        name="ccm_layer",
    )(x, row(g), w_in, row(b_in), dw_w, row(dw_b), row(ln_g), row(ln_b), w_out, row(b_out),
      row(final_g), conv0)


def kernel(x_prompt, x_sample, state_lru_conv, state_lru_h, state_ccm_conv, norm_g, final_norm_g,
           lru_w_in, lru_conv_w, lru_conv_b, lru_w_a, lru_b_a, lru_w_i, lru_b_i, lru_lam, lru_w_out,
           ccm_w_in, ccm_b_in, ccm_dw_w, ccm_dw_b, ccm_ln_g, ccm_ln_b, ccm_w_out, ccm_b_out):
    depth = norm_g.shape[0]
    bp = x_prompt.shape[0]
    c = lru_w_out.shape[1]
    assert depth % 2 == 0
    tm = lambda a: jnp.swapaxes(a, -3, -2)

    xp = tm(x_prompt)
    xs = tm(x_sample)
    s_lru_conv = tm(state_lru_conv)
    s_ccm_conv = tm(state_ccm_conv)
    z_lru_conv = jnp.zeros((LRU_CONV_W - 1, bp, c), F32)
    z_lru_h = jnp.zeros((bp, c), F32)
    z_ccm_conv = jnp.zeros((CCM_CONV_W - 1, bp, c), F32)

    lru_w_in_b = _pack_rows(lru_w_in)
    lru_w_out_b = _pack_rows(lru_w_out)
    lru_w_gate_b = _pack_rows(jnp.concatenate([lru_w_a, lru_w_i], axis=-1))
    ccm_w_in_b = _pack_rows(ccm_w_in)
    ccm_w_out_b = _pack_rows(ccm_w_out)

    lru_conv_p, lru_h_p, ccm_conv_p = [], [], []
    lru_conv_s, lru_h_s, ccm_conv_s = [], [], []
    for l in range(depth):
        j = l // 2
        if l % 2 == 0:
            lw = (norm_g[l], lru_w_in_b[j], lru_conv_w[j], lru_conv_b[j], lru_w_gate_b[j],
                  lru_b_a[j], lru_b_i[j], lru_lam[j], lru_w_out_b[j])
            xp, cb, hl = _lru_layer(xp, z_lru_conv, z_lru_h, *lw, tt=32, bt=bp)
            lru_conv_p.append(cb)
            lru_h_p.append(hl)
            xs, cb, hl = _lru_layer(xs, s_lru_conv[j], state_lru_h[j], *lw, tt=xs.shape[0], bt=64)
            lru_conv_s.append(cb)
            lru_h_s.append(hl)
        else:
            last = l == depth - 1
            cw = (norm_g[l], ccm_w_in_b[j], ccm_b_in[j], ccm_dw_w[j], ccm_dw_b[j], ccm_ln_g[j],
                  ccm_ln_b[j], ccm_w_out_b[j], ccm_b_out[j], final_norm_g)
            xp, cb = _ccm_layer(xp, z_ccm_conv, *cw, tt=32, bt=bp, final_norm=last)
            ccm_conv_p.append(cb)
            xs, cb = _ccm_layer(xs, s_ccm_conv[j], *cw, tt=xs.shape[0], bt=16, final_norm=last)
            ccm_conv_s.append(cb)
    return (tm(xp), tm(xs),
            tm(jnp.stack(lru_conv_p)), jnp.stack(lru_h_p), tm(jnp.stack(ccm_conv_p)),
            tm(jnp.stack(lru_conv_s)), jnp.stack(lru_h_s), tm(jnp.stack(ccm_conv_s)))
```

```python
import functools

import jax
import jax.numpy as jnp
from jax import lax
from jax.experimental import pallas as pl
from jax.experimental.pallas import tpu as pltpu

F32 = jnp.float32
BF16 = jnp.bfloat16

RMS_EPS = 1e-6
LN_EPS = 1e-5
LRU_C = 8.0
LRU_BLOCK = 128
LRU_CONV_W = 4
CCM_CONV_W = 31

V7X_VMEM_BYTES = 64 * 1024 * 1024
SUBLANES = 8
LANES = 128
VMEM_LIMIT_BYTES = V7X_VMEM_BYTES - 8 * 1024 * 1024

NORM_ROW_TILE = 32
CONV_ACC_VREGS = 16
MXU_COLS = 256
MIN_OUT_ROWS = 128
PACK_BLOCK_ROWS = 512
PACK_BLOCK_COLS = 1024


def _rmsnorm(x, g):
    ms = jnp.mean(x * x, axis=-1, keepdims=True)
    return (x * lax.rsqrt(ms + RMS_EPS)) * g


def _silu(x):
    return x * jax.nn.sigmoid(x)


def _bf16_bits(x):
    u = pltpu.bitcast(x, jnp.uint32)
    return (u + jnp.uint32(0x7FFF) + ((u >> 16) & jnp.uint32(1))) >> 16


def _pack_kernel(w_ref, o_ref, stage_s):
    kb = w_ref.shape[1]
    for j in range(w_ref.shape[2] // LANES):
        cols = slice(j * LANES, (j + 1) * LANES)
        stage_s[j] = w_ref[0, :, cols]
        lo = _bf16_bits(stage_s[j, pl.ds(0, kb // 2, stride=2), :])
        hi = _bf16_bits(stage_s[j, pl.ds(1, kb // 2, stride=2), :])
        o_ref[0, :, cols] = lo | (hi << 16)


def _pack_rows(w):
    *lead, k, n = w.shape
    w3 = w.reshape(-1, k, n)
    kb, nb = min(k, PACK_BLOCK_ROWS), min(n, PACK_BLOCK_COLS)
    assert k % kb == 0 and n % nb == 0 and kb % (2 * SUBLANES) == 0 and nb % LANES == 0
    packed = pl.pallas_call(
        _pack_kernel,
        grid=(w3.shape[0], k // kb, n // nb),
        in_specs=[pl.BlockSpec((1, kb, nb), lambda i, a, b: (i, a, b))],
        out_specs=pl.BlockSpec((1, kb // 2, nb), lambda i, a, b: (i, a, b)),
        out_shape=jax.ShapeDtypeStruct((w3.shape[0], k // 2, n), jnp.uint32),
        scratch_shapes=[pltpu.VMEM((nb // LANES, kb, LANES), F32)],
        name="pack_weights",
    )(w3)
    return packed.reshape(*lead, k // 2, n)


def _rows_bf16(packed):
    return pltpu.bitcast(packed, BF16)


def _interleave(front, back):
    merged, i = [], 0
    for j, item in enumerate(back):
        while i < len(front) and i * len(back) <= j * len(front):
            merged.append(front[i])
            i += 1
        merged.append(item)
    merged.extend(front[i:])
    return merged


def _run_halves(front, back, pipelined):
    for part in (_interleave(front, back) if pipelined else front + back):
        part()


def _lru_kernel(x_ref, g_ref, win_ref, cw_ref, cb_ref, wg_ref, ba_ref, bi_ref,
                lam_ref, wout_ref, conv0_ref, h0_ref,
                xo_ref, convo_ref, ho_ref,
                xp_s, z_s, xr_s, hn_s, y_s, acc_s, h_s, *, tt, bt, nt):
    d = x_ref.shape[-1]
    c = xp_s.shape[-1]
    r = tt * bt
    p = (LRU_CONV_W - 1) * bt
    pipelined = nt > 1
    nsteps = nt + 1 if pipelined else nt
    xb_off, xa_off = (p, p + r) if pipelined else (p, p)
    zb_off, za_off = (0, r) if pipelined else (0, 0)
    it = pl.program_id(1)

    @pl.when(it == 0)
    def _():
        if pipelined:
            xp_s[0:p + r, :] = jnp.zeros((p + r, c), F32)
            z_s[0:r, :] = jnp.zeros((r, c), F32)
            xr_s[0:r, :] = jnp.zeros((r, d), F32)
            h_s[...] = jnp.zeros(h_s.shape, F32)
        else:
            xp_s[0:p, :] = conv0_ref[...].reshape(p, c)
            h_s[...] = h0_ref[...]

    def front_norm():
        x = x_ref[...].reshape(r, d)
        xr_s[za_off:za_off + r, :] = x
        hn_s[...] = _rmsnorm(x, g_ref[...]).astype(BF16)

    def front_x(j):
        cols = slice(j * MXU_COLS, (j + 1) * MXU_COLS)
        xp_s[xa_off:xa_off + r, cols] = jnp.dot(hn_s[...], _rows_bf16(win_ref[:, cols]), preferred_element_type=F32)

    def front_z(j):
        cols = slice(j * MXU_COLS, (j + 1) * MXU_COLS)
        wcols = slice(c + j * MXU_COLS, c + (j + 1) * MXU_COLS)
        z_s[za_off:za_off + r, cols] = _silu(jnp.dot(hn_s[...], _rows_bf16(win_ref[:, wcols]), preferred_element_type=F32))

    def back_head(h):
        cols = slice(h * LRU_BLOCK, (h + 1) * LRU_BLOCK)
        base = xb_off - p
        xc = cb_ref[:, cols] + cw_ref[0:1, cols] * xp_s[base:base + r, cols]
        for k in range(1, LRU_CONV_W):
            xc = xc + cw_ref[k:k + 1, cols] * xp_s[base + k * bt:base + k * bt + r, cols]
        gates = jnp.dot(xc.astype(BF16), _rows_bf16(wg_ref[h]), preferred_element_type=F32)
        rg = jax.nn.sigmoid(gates[:, 0:LRU_BLOCK] + ba_ref[:, cols])
        ig = jax.nn.sigmoid(gates[:, LRU_BLOCK:] + bi_ref[:, cols])
        log_a = (-LRU_C * rg) * jax.nn.softplus(-lam_ref[:, cols])
        a = jnp.exp(log_a)
        b = jnp.sqrt(-jnp.tanh(log_a) * (a * a + 1.0)) * (ig * xc)
        hcur = h_s[:, cols]
        hs = []
        for t in range(tt):
            hcur = a[t * bt:(t + 1) * bt] * hcur + b[t * bt:(t + 1) * bt]
            hs.append(hcur)
        h_s[:, cols] = hcur
        hall = jnp.concatenate(hs, axis=0)
        y_s[:, cols] = (hall * z_s[zb_off:zb_off + r, cols]).astype(BF16)

    def back_out(j):
        rows = slice(j * MXU_COLS, (j + 1) * MXU_COLS)
        wrows = slice(j * MXU_COLS // 2, (j + 1) * MXU_COLS // 2)
        part = jnp.dot(y_s[:, rows], _rows_bf16(wout_ref[wrows, :]), preferred_element_type=F32)
        if j == 0:
            acc_s[...] = xr_s[zb_off:zb_off + r, :] + part
        elif j < c // MXU_COLS - 1:
            acc_s[...] = acc_s[...] + part
        else:
            xo_ref[...] = (acc_s[...] + part).reshape(tt, bt, d)

    heads_per_chunk = MXU_COLS // LRU_BLOCK
    front = [front_norm]
    for j in range(c // MXU_COLS):
        front += [functools.partial(front_x, j), functools.partial(front_z, j)]
    back = []
    for j in range(c // MXU_COLS):
        back += [functools.partial(back_head, j * heads_per_chunk + i) for i in range(heads_per_chunk)]
        back.append(functools.partial(back_out, j))
    _run_halves(front, back, pipelined)

    if pipelined:
        xp_s[0:p + r, :] = xp_s[r:p + 2 * r, :]
        z_s[0:r, :] = z_s[r:2 * r, :]
        xr_s[0:r, :] = xr_s[r:2 * r, :]

        @pl.when(it == 0)
        def _():
            xp_s[0:p, :] = conv0_ref[...].reshape(p, c)
            h_s[...] = h0_ref[...]

    @pl.when(it == nsteps - 1)
    def _():
        tail = 0 if pipelined else r
        convo_ref[...] = xp_s[tail:tail + p, :].reshape(LRU_CONV_W - 1, bt, c)
        ho_ref[...] = h_s[...]


def _ccm_kernel(x_ref, g_ref, win_ref, bin_ref, dw_ref, dwb_ref, lng_ref, lnb_ref,
                wout_ref, bout_ref, fg_ref, conv0_ref,
                xo_ref, convo_ref,
                cp_s, z_s, y_s, s_s, wb_s, *, tt, bt, nt, final_norm):
    d = x_ref.shape[-1]
    c = z_s.shape[-1]
    r = tt * bt
    p = (CCM_CONV_W - 1) * bt
    it = pl.program_id(1)

    @pl.when(it == 0)
    def _():
        for k in range(CCM_CONV_W):
            wb_s[k] = jnp.broadcast_to(dw_ref[k:k + 1, :], (SUBLANES, c))
        cp_s[0:p, :] = conv0_ref[...].reshape(p, c)

    def lane_chunk(start, width):
        return pl.ds(pl.multiple_of(start, width), width)

    tg = min(tt, CONV_ACC_VREGS * SUBLANES // bt)

    def conv_chunk(jc, carry):
        cols = lane_chunk(jc * LANES, LANES)
        w = [wb_s[k, :, cols] for k in range(CCM_CONV_W)]
        bias = jnp.broadcast_to(dwb_ref[:, cols], (bt, LANES))
        for g in range(tt // tg):
            accs = [bias] * tg
            row0 = g * tg * bt
            for s in range(tg + CCM_CONV_W - 1):
                xs = cp_s[row0 + s * bt:row0 + (s + 1) * bt, cols]
                xs = xs.reshape(bt // SUBLANES, SUBLANES, LANES)
                for t in range(max(0, s - (CCM_CONV_W - 1)), min(tg, s + 1)):
                    accs[t] = accs[t] + (xs * w[s - t][None]).reshape(bt, LANES)
            for t in range(tg):
                y_s[(g * tg + t) * bt:(g * tg + t + 1) * bt, cols] = accs[t]
        return carry

    rt = NORM_ROW_TILE

    def norm_tile(i):
        r0 = i * rt
        y = y_s[r0:r0 + rt, :]
        mu = jnp.mean(y, axis=-1, keepdims=True)
        yc = y - mu
        var = jnp.mean(yc * yc, axis=-1, keepdims=True)
        yn = yc * lax.rsqrt(var + LN_EPS) * lng_ref[...] + lnb_ref[...]
        s_s[r0:r0 + rt, :] = (_silu(yn) * z_s[r0:r0 + rt, :]).astype(BF16)

    def out_rows(r0, nrows):
        out = jnp.dot(s_s[r0:r0 + nrows, :], _rows_bf16(wout_ref[...]), preferred_element_type=F32) + bout_ref[...]
        xn = x_ref[r0 // bt:(r0 + nrows) // bt].reshape(nrows, d) + out
        if final_norm:
            xn = _rmsnorm(xn, fg_ref[...])
        xo_ref[r0 // bt:(r0 + nrows) // bt] = xn.reshape(nrows // bt, bt, d)

    hn = _rmsnorm(x_ref[...].reshape(r, d), g_ref[...]).astype(BF16)

    def in_proj(part):
        wcols = slice(part * c, (part + 1) * c)
        return jnp.dot(hn, _rows_bf16(win_ref[:, wcols]), preferred_element_type=F32) + bin_ref[:, wcols]

    cp_s[p:p + r, :] = in_proj(0) * jax.nn.sigmoid(in_proj(1))
    z_s[...] = _silu(in_proj(2))
    lax.fori_loop(0, c // LANES, conv_chunk, 0)
    group = min(r, max(r // 2, MIN_OUT_ROWS))
    for i in range(r // rt):
        norm_tile(i)
        if ((i + 1) * rt) % group == 0:
            out_rows((i + 1) * rt - group, group)

    @pl.when(it == nt - 1)
    def _():
        convo_ref[...] = cp_s[r:r + p, :].reshape(CCM_CONV_W - 1, bt, c)

    if nt > 1:
        cp_s[0:p, :] = cp_s[r:r + p, :]


def _const_spec(shape):
    nd = len(shape)
    return pl.BlockSpec(shape, lambda ib, it, nd=nd: (0,) * nd, pipeline_mode=pl.Buffered(1))


def _stream_specs(tt, bt, d, nt):
    if nt > 1:
        in_map = lambda ib, it: (jnp.minimum(it, nt - 1), ib, 0)
        out_map = lambda ib, it: (jnp.maximum(it - 1, 0), ib, 0)
        nsteps = nt + 1
    else:
        in_map = out_map = lambda ib, it: (it, ib, 0)
        nsteps = nt
    return pl.BlockSpec((tt, bt, d), in_map), pl.BlockSpec((tt, bt, d), out_map), nsteps


def _compiler_params():
    return pltpu.CompilerParams(
        dimension_semantics=("arbitrary", "arbitrary"),
        vmem_limit_bytes=VMEM_LIMIT_BYTES)


def _lru_layer(x, conv0, h0, g, w_in, conv_w, conv_b, w_gate, b_a, b_i, lam, w_out, *, tt, bt):
    t, b, d = x.shape
    c = conv_w.shape[-1]
    assert w_in.shape == (d // 2, 2 * c) and w_out.shape == (c // 2, d)
    nt = t // tt
    nb = b // bt
    r = tt * bt
    p = (LRU_CONV_W - 1) * bt
    assert t % tt == 0 and b % bt == 0 and bt % SUBLANES == 0 and r % (2 * SUBLANES) == 0
    assert nt == 1 or r >= p
    assert c % MXU_COLS == 0 and c // MXU_COLS >= 2 and MXU_COLS % LRU_BLOCK == 0
    nblk = 2 if nt > 1 else 1
    x_in, x_out, nsteps = _stream_specs(tt, bt, d, nt)
    kern = functools.partial(_lru_kernel, tt=tt, bt=bt, nt=nt)
    row = lambda a: a.reshape(1, -1)
    return pl.pallas_call(
        kern,
        grid=(nb, nsteps),
        in_specs=[
            x_in,
            _const_spec((1, d)),
            _const_spec((d // 2, 2 * c)),
            _const_spec((LRU_CONV_W, c)),
            _const_spec((1, c)),
            _const_spec(w_gate.shape),
            _const_spec((1, c)),
            _const_spec((1, c)),
            _const_spec((1, c)),
            _const_spec((c // 2, d)),
            pl.BlockSpec((LRU_CONV_W - 1, bt, c), lambda ib, it: (0, ib, 0)),
            pl.BlockSpec((bt, c), lambda ib, it: (ib, 0)),
        ],
        out_specs=[
            x_out,
            pl.BlockSpec((LRU_CONV_W - 1, bt, c), lambda ib, it: (0, ib, 0)),
            pl.BlockSpec((bt, c), lambda ib, it: (ib, 0)),
        ],
        out_shape=[
            jax.ShapeDtypeStruct((t, b, d), F32),
            jax.ShapeDtypeStruct((LRU_CONV_W - 1, b, c), F32),
            jax.ShapeDtypeStruct((b, c), F32),
        ],
        scratch_shapes=[
            pltpu.VMEM((p + nblk * r, c), F32),
            pltpu.VMEM((nblk * r, c), F32),
            pltpu.VMEM((nblk * r, d), F32),
            pltpu.VMEM((r, d), BF16),
            pltpu.VMEM((r, c), BF16),
            pltpu.VMEM((r, d), F32),
            pltpu.VMEM((bt, c), F32),
        ],
        compiler_params=_compiler_params(),
        name="lru_layer",
    )(x, row(g), w_in, conv_w, row(conv_b), w_gate, row(b_a), row(b_i), row(lam), w_out, conv0, h0)


def _ccm_layer(x, conv0, g, w_in, b_in, dw_w, dw_b, ln_g, ln_b, w_out, b_out, final_g, *,
               tt, bt, final_norm):
    t, b, d = x.shape
    c = dw_w.shape[-1]
    assert w_in.shape == (d // 2, 3 * c) and w_out.shape == (c // 2, d)
    nt = t // tt
    nb = b // bt
    r = tt * bt
    p = (CCM_CONV_W - 1) * bt
    tg = min(tt, CONV_ACC_VREGS * SUBLANES // bt)
    assert t % tt == 0 and b % bt == 0 and bt % SUBLANES == 0
    assert tt % tg == 0 and r % NORM_ROW_TILE == 0 and c % MXU_COLS == 0
    assert nt == 1 or r >= p
    assert r % min(r, max(r // 2, MIN_OUT_ROWS)) == 0
    kern = functools.partial(_ccm_kernel, tt=tt, bt=bt, nt=nt, final_norm=final_norm)
    row = lambda a: a.reshape(1, -1)
    x_spec = pl.BlockSpec((tt, bt, d), lambda ib, it: (it, ib, 0))
    return pl.pallas_call(
        kern,
        grid=(nb, nt),
        in_specs=[
            x_spec,
            _const_spec((1, d)),
            _const_spec((d // 2, 3 * c)),
            _const_spec((1, 3 * c)),
            _const_spec((CCM_CONV_W, c)),
            _const_spec((1, c)),
            _const_spec((1, c)),
            _const_spec((1, c)),
            _const_spec((c // 2, d)),
            _const_spec((1, d)),
            _const_spec((1, d)),
            pl.BlockSpec((CCM_CONV_W - 1, bt, c), lambda ib, it: (0, ib, 0)),
        ],
        out_specs=[
            x_spec,
            pl.BlockSpec((CCM_CONV_W - 1, bt, c), lambda ib, it: (0, ib, 0)),
        ],
        out_shape=[
            jax.ShapeDtypeStruct((t, b, d), F32),
            jax.ShapeDtypeStruct((CCM_CONV_W - 1, b, c), F32),
        ],
        scratch_shapes=[
            pltpu.VMEM((p + r, c), F32),
            pltpu.VMEM((r, c), F32),
            pltpu.VMEM((r, c), F32),
            pltpu.VMEM((r, c), BF16),
            pltpu.VMEM((CCM_CONV_W, SUBLANES, c), F32),
        ],
        compiler_params=_compiler_params(),
        name="ccm_layer",
    )(x, row(g), w_in, row(b_in), dw_w, row(dw_b), row(ln_g), row(ln_b), w_out, row(b_out),
      row(final_g), conv0)


def kernel(x_prompt, x_sample, state_lru_conv, state_lru_h, state_ccm_conv, norm_g, final_norm_g,
           lru_w_in, lru_conv_w, lru_conv_b, lru_w_a, lru_b_a, lru_w_i, lru_b_i, lru_lam, lru_w_out,
           ccm_w_in, ccm_b_in, ccm_dw_w, ccm_dw_b, ccm_ln_g, ccm_ln_b, ccm_w_out, ccm_b_out):
    depth = norm_g.shape[0]
    bp = x_prompt.shape[0]
    c = lru_w_out.shape[1]
    assert depth % 2 == 0
    tm = lambda a: jnp.swapaxes(a, -3, -2)

    xp = tm(x_prompt)
    xs = tm(x_sample)
    s_lru_conv = tm(state_lru_conv)
    s_ccm_conv = tm(state_ccm_conv)
    z_lru_conv = jnp.zeros((LRU_CONV_W - 1, bp, c), F32)
    z_lru_h = jnp.zeros((bp, c), F32)
    z_ccm_conv = jnp.zeros((CCM_CONV_W - 1, bp, c), F32)

    lru_w_in_b = _pack_rows(lru_w_in)
    lru_w_out_b = _pack_rows(lru_w_out)
    lru_w_gate_b = _pack_rows(jnp.concatenate([lru_w_a, lru_w_i], axis=-1))
    ccm_w_in_b = _pack_rows(ccm_w_in)
    ccm_w_out_b = _pack_rows(ccm_w_out)

    lru_conv_p, lru_h_p, ccm_conv_p = [], [], []
    lru_conv_s, lru_h_s, ccm_conv_s = [], [], []
    for l in range(depth):
        j = l // 2
        if l % 2 == 0:
            lw = (norm_g[l], lru_w_in_b[j], lru_conv_w[j], lru_conv_b[j], lru_w_gate_b[j],
                  lru_b_a[j], lru_b_i[j], lru_lam[j], lru_w_out_b[j])
            xp, cb, hl = _lru_layer(xp, z_lru_conv, z_lru_h, *lw, tt=32, bt=bp)
            lru_conv_p.append(cb)
            lru_h_p.append(hl)
            xs, cb, hl = _lru_layer(xs, s_lru_conv[j], state_lru_h[j], *lw, tt=xs.shape[0], bt=64)
            lru_conv_s.append(cb)
            lru_h_s.append(hl)
        else:
            last = l == depth - 1
            cw = (norm_g[l], ccm_w_in_b[j], ccm_b_in[j], ccm_dw_w[j], ccm_dw_b[j], ccm_ln_g[j],
                  ccm_ln_b[j], ccm_w_out_b[j], ccm_b_out[j], final_norm_g)
            xp, cb = _ccm_layer(xp, z_ccm_conv, *cw, tt=32, bt=bp, final_norm=last)
            ccm_conv_p.append(cb)
            xs, cb = _ccm_layer(xs, s_ccm_conv[j], *cw, tt=xs.shape[0], bt=16, final_norm=last)
            ccm_conv_s.append(cb)
    return (tm(xp), tm(xs),
            tm(jnp.stack(lru_conv_p)), jnp.stack(lru_h_p), tm(jnp.stack(ccm_conv_p)),
            tm(jnp.stack(lru_conv_s)), jnp.stack(lru_h_s), tm(jnp.stack(ccm_conv_s)))
```

```python
import functools

import jax
import jax.numpy as jnp
from jax import lax
from jax.experimental import pallas as pl
from jax.experimental.pallas import tpu as pltpu

F32 = jnp.float32
BF16 = jnp.bfloat16

RMS_EPS = 1e-6
LN_EPS = 1e-5
LRU_C = 8.0
LRU_BLOCK = 128
LRU_CONV_W = 4
CCM_CONV_W = 31

V7X_VMEM_BYTES = 64 * 1024 * 1024
SUBLANES = 8
LANES = 128
VMEM_LIMIT_BYTES = V7X_VMEM_BYTES - 8 * 1024 * 1024

NORM_ROW_TILE = 32
CONV_ACC_VREGS = 16
MXU_COLS = 256
MIN_OUT_ROWS = 128
PACK_BLOCK_ROWS = 1024
PACK_BLOCK_COLS = 1024


def _rmsnorm(x, g):
    ms = jnp.mean(x * x, axis=-1, keepdims=True)
    return (x * lax.rsqrt(ms + RMS_EPS)) * g


def _silu(x):
    return x * jax.nn.sigmoid(x)


def _bf16_bits(x):
    u = pltpu.bitcast(x, jnp.uint32)
    return (u + jnp.uint32(0x7FFF) + ((u >> 16) & jnp.uint32(1))) >> 16


def _pack_kernel(w_ref, o_ref, stage_s):
    kb = w_ref.shape[0]
    for j in range(w_ref.shape[1] // LANES):
        cols = slice(j * LANES, (j + 1) * LANES)
        stage_s[j] = w_ref[:, cols]
        lo = _bf16_bits(stage_s[j, pl.ds(0, kb // 2, stride=2), :])
        hi = _bf16_bits(stage_s[j, pl.ds(1, kb // 2, stride=2), :])
        o_ref[:, cols] = lo | (hi << 16)


def _pack_rows(w):
    *lead, k, n = w.shape
    assert k % 2 == 0
    w2 = w.reshape(-1, n)
    rows = w2.shape[0]
    kb, nb = min(rows, PACK_BLOCK_ROWS), min(n, PACK_BLOCK_COLS)
    assert rows % kb == 0 and n % nb == 0 and kb % (2 * SUBLANES) == 0 and nb % LANES == 0
    packed = pl.pallas_call(
        _pack_kernel,
        grid=(rows // kb, n // nb),
        in_specs=[pl.BlockSpec((kb, nb), lambda a, b: (a, b))],
        out_specs=pl.BlockSpec((kb // 2, nb), lambda a, b: (a, b)),
        out_shape=jax.ShapeDtypeStruct((rows // 2, n), jnp.uint32),
        scratch_shapes=[pltpu.VMEM((nb // LANES, kb, LANES), F32)],
        name="pack_weights",
    )(w2)
    return packed.reshape(*lead, k // 2, n)


def _rows_bf16(packed):
    return pltpu.bitcast(packed, BF16)


def _interleave(front, back):
    merged, i = [], 0
    for j, item in enumerate(back):
        while i < len(front) and i * len(back) <= j * len(front):
            merged.append(front[i])
            i += 1
        merged.append(item)
    merged.extend(front[i:])
    return merged


def _run_halves(front, back, pipelined):
    for part in (_interleave(front, back) if pipelined else front + back):
        part()


def _lru_kernel(x_ref, g_ref, win_ref, cw_ref, cb_ref, wg_ref, ba_ref, bi_ref,
                lam_ref, wout_ref, conv0_ref, h0_ref,
                xo_ref, convo_ref, ho_ref,
                xp_s, z_s, xr_s, hn_s, y_s, acc_s, h_s, *, tt, bt, nt, batch_major_in):
    d = x_ref.shape[-1]
    c = xp_s.shape[-1]
    r = tt * bt
    p = (LRU_CONV_W - 1) * bt
    pipelined = nt > 1
    nsteps = nt + 1 if pipelined else nt
    xb_off, xa_off = (p, p + r) if pipelined else (p, p)
    zb_off, za_off = (0, r) if pipelined else (0, 0)
    it = pl.program_id(1)

    @pl.when(it == 0)
    def _():
        if pipelined:
            xp_s[0:p + r, :] = jnp.zeros((p + r, c), F32)
            z_s[0:r, :] = jnp.zeros((r, c), F32)
            xr_s[0:r, :] = jnp.zeros((r, d), F32)
            h_s[...] = jnp.zeros(h_s.shape, F32)
        else:
            xp_s[0:p, :] = conv0_ref[...].reshape(p, c)
            h_s[...] = h0_ref[...]

    def front_norm():
        if batch_major_in:
            x = pltpu.einshape("btd->(tb)d", x_ref[...])
        else:
            x = x_ref[...].reshape(r, d)
        xr_s[za_off:za_off + r, :] = x
        hn_s[...] = _rmsnorm(x, g_ref[...]).astype(BF16)

    def front_x(j):
        cols = slice(j * MXU_COLS, (j + 1) * MXU_COLS)
        xp_s[xa_off:xa_off + r, cols] = jnp.dot(hn_s[...], _rows_bf16(win_ref[:, cols]), preferred_element_type=F32)

    def front_z(j):
        cols = slice(j * MXU_COLS, (j + 1) * MXU_COLS)
        wcols = slice(c + j * MXU_COLS, c + (j + 1) * MXU_COLS)
        z_s[za_off:za_off + r, cols] = _silu(jnp.dot(hn_s[...], _rows_bf16(win_ref[:, wcols]), preferred_element_type=F32))

    def back_head(h):
        cols = slice(h * LRU_BLOCK, (h + 1) * LRU_BLOCK)
        base = xb_off - p
        xc = cb_ref[:, cols] + cw_ref[0:1, cols] * xp_s[base:base + r, cols]
        for k in range(1, LRU_CONV_W):
            xc = xc + cw_ref[k:k + 1, cols] * xp_s[base + k * bt:base + k * bt + r, cols]
        gates = jnp.dot(xc.astype(BF16), _rows_bf16(wg_ref[h]), preferred_element_type=F32)
        rg = jax.nn.sigmoid(gates[:, 0:LRU_BLOCK] + ba_ref[:, cols])
        ig = jax.nn.sigmoid(gates[:, LRU_BLOCK:] + bi_ref[:, cols])
        log_a = (-LRU_C * rg) * jax.nn.softplus(-lam_ref[:, cols])
        a = jnp.exp(log_a)
        b = jnp.sqrt(-jnp.tanh(log_a) * (a * a + 1.0)) * (ig * xc)
        hcur = h_s[:, cols]
        hs = []
        for t in range(tt):
            hcur = a[t * bt:(t + 1) * bt] * hcur + b[t * bt:(t + 1) * bt]
            hs.append(hcur)
        h_s[:, cols] = hcur
        hall = jnp.concatenate(hs, axis=0)
        y_s[:, cols] = (hall * z_s[zb_off:zb_off + r, cols]).astype(BF16)

    def back_out(j):
        rows = slice(j * MXU_COLS, (j + 1) * MXU_COLS)
        wrows = slice(j * MXU_COLS // 2, (j + 1) * MXU_COLS // 2)
        part = jnp.dot(y_s[:, rows], _rows_bf16(wout_ref[wrows, :]), preferred_element_type=F32)
        if j == 0:
            acc_s[...] = xr_s[zb_off:zb_off + r, :] + part
        elif j < c // MXU_COLS - 1:
            acc_s[...] = acc_s[...] + part
        else:
            xo_ref[...] = (acc_s[...] + part).reshape(tt, bt, d)

    heads_per_chunk = MXU_COLS // LRU_BLOCK
    front = [front_norm]
    for j in range(c // MXU_COLS):
        front += [functools.partial(front_x, j), functools.partial(front_z, j)]
    back = []
    for j in range(c // MXU_COLS):
        back += [functools.partial(back_head, j * heads_per_chunk + i) for i in range(heads_per_chunk)]
        back.append(functools.partial(back_out, j))
    _run_halves(front, back, pipelined)

    if pipelined:
        xp_s[0:p + r, :] = xp_s[r:p + 2 * r, :]
        z_s[0:r, :] = z_s[r:2 * r, :]
        xr_s[0:r, :] = xr_s[r:2 * r, :]

        @pl.when(it == 0)
        def _():
            xp_s[0:p, :] = conv0_ref[...].reshape(p, c)
            h_s[...] = h0_ref[...]

    @pl.when(it == nsteps - 1)
    def _():
        tail = 0 if pipelined else r
        convo_ref[...] = xp_s[tail:tail + p, :].reshape(LRU_CONV_W - 1, bt, c)
        ho_ref[...] = h_s[...]


def _ccm_kernel(x_ref, g_ref, win_ref, bin_ref, dw_ref, dwb_ref, lng_ref, lnb_ref,
                wout_ref, bout_ref, fg_ref, conv0_ref,
                xo_ref, convo_ref,
                cp_s, z_s, y_s, s_s, wb_s, *, tt, bt, nt, final_norm, batch_major_out):
    d = x_ref.shape[-1]
    c = z_s.shape[-1]
    r = tt * bt
    p = (CCM_CONV_W - 1) * bt
    it = pl.program_id(1)

    @pl.when(it == 0)
    def _():
        for k in range(CCM_CONV_W):
            wb_s[k] = jnp.broadcast_to(dw_ref[k:k + 1, :], (SUBLANES, c))
        cp_s[0:p, :] = conv0_ref[...].reshape(p, c)

    def lane_chunk(start, width):
        return pl.ds(pl.multiple_of(start, width), width)

    tg = min(tt, CONV_ACC_VREGS * SUBLANES // bt)

    def conv_chunk(jc, carry):
        cols = lane_chunk(jc * LANES, LANES)
        w = [wb_s[k, :, cols] for k in range(CCM_CONV_W)]
        bias = jnp.broadcast_to(dwb_ref[:, cols], (bt, LANES))
        for g in range(tt // tg):
            accs = [bias] * tg
            row0 = g * tg * bt
            for s in range(tg + CCM_CONV_W - 1):
                xs = cp_s[row0 + s * bt:row0 + (s + 1) * bt, cols]
                xs = xs.reshape(bt // SUBLANES, SUBLANES, LANES)
                for t in range(max(0, s - (CCM_CONV_W - 1)), min(tg, s + 1)):
                    accs[t] = accs[t] + (xs * w[s - t][None]).reshape(bt, LANES)
            for t in range(tg):
                y_s[(g * tg + t) * bt:(g * tg + t + 1) * bt, cols] = accs[t]
        return carry

    rt = NORM_ROW_TILE

    def norm_tile(i):
        r0 = i * rt
        y = y_s[r0:r0 + rt, :]
        mu = jnp.mean(y, axis=-1, keepdims=True)
        yc = y - mu
        var = jnp.mean(yc * yc, axis=-1, keepdims=True)
        yn = yc * lax.rsqrt(var + LN_EPS) * lng_ref[...] + lnb_ref[...]
        s_s[r0:r0 + rt, :] = (_silu(yn) * z_s[r0:r0 + rt, :]).astype(BF16)

    def out_rows(r0, nrows):
        out = jnp.dot(s_s[r0:r0 + nrows, :], _rows_bf16(wout_ref[...]), preferred_element_type=F32) + bout_ref[...]
        xn = x_ref[r0 // bt:(r0 + nrows) // bt].reshape(nrows, d) + out
        if final_norm:
            xn = _rmsnorm(xn, fg_ref[...])
        if batch_major_out:
            xo_ref[:, r0 // bt:(r0 + nrows) // bt, :] = pltpu.einshape("(tb)d->btd", xn, b=bt)
        else:
            xo_ref[r0 // bt:(r0 + nrows) // bt] = xn.reshape(nrows // bt, bt, d)

    hn = _rmsnorm(x_ref[...].reshape(r, d), g_ref[...]).astype(BF16)

    def in_proj(part):
        wcols = slice(part * c, (part + 1) * c)
        return jnp.dot(hn, _rows_bf16(win_ref[:, wcols]), preferred_element_type=F32) + bin_ref[:, wcols]

    cp_s[p:p + r, :] = in_proj(0) * jax.nn.sigmoid(in_proj(1))
    z_s[...] = _silu(in_proj(2))
    lax.fori_loop(0, c // LANES, conv_chunk, 0)
    group = min(r, max(r // 2, MIN_OUT_ROWS))
    for i in range(r // rt):
        norm_tile(i)
        if ((i + 1) * rt) % group == 0:
            out_rows((i + 1) * rt - group, group)

    @pl.when(it == nt - 1)
    def _():
        convo_ref[...] = cp_s[r:r + p, :].reshape(CCM_CONV_W - 1, bt, c)

    if nt > 1:
        cp_s[0:p, :] = cp_s[r:r + p, :]


def _const_spec(shape):
    nd = len(shape)
    return pl.BlockSpec(shape, lambda ib, it, nd=nd: (0,) * nd, pipeline_mode=pl.Buffered(1))


def _stream_specs(tt, bt, d, nt, batch_major_in):
    if nt > 1:
        in_map = lambda ib, it: (jnp.minimum(it, nt - 1), ib, 0)
        out_map = lambda ib, it: (jnp.maximum(it - 1, 0), ib, 0)
        nsteps = nt + 1
    else:
        in_map = out_map = lambda ib, it: (it, ib, 0)
        nsteps = nt
    if batch_major_in:
        x_in = pl.BlockSpec((bt, tt, d), lambda ib, it: tuple(in_map(ib, it)[i] for i in (1, 0, 2)))
    else:
        x_in = pl.BlockSpec((tt, bt, d), in_map)
    return x_in, pl.BlockSpec((tt, bt, d), out_map), nsteps


def _compiler_params():
    return pltpu.CompilerParams(
        dimension_semantics=("arbitrary", "arbitrary"),
        vmem_limit_bytes=VMEM_LIMIT_BYTES)


def _lru_layer(x, conv0, h0, g, w_in, conv_w, conv_b, w_gate, b_a, b_i, lam, w_out, *, tt, bt,
               batch_major_in=False):
    (b, t, d) = x.shape if batch_major_in else (x.shape[1], x.shape[0], x.shape[2])
    c = conv_w.shape[-1]
    assert w_in.shape == (d // 2, 2 * c) and w_out.shape == (c // 2, d)
    nt = t // tt
    nb = b // bt
    r = tt * bt
    p = (LRU_CONV_W - 1) * bt
    assert t % tt == 0 and b % bt == 0 and bt % SUBLANES == 0 and r % (2 * SUBLANES) == 0
    assert nt == 1 or r >= p
    assert c % MXU_COLS == 0 and c // MXU_COLS >= 2 and MXU_COLS % LRU_BLOCK == 0
    nblk = 2 if nt > 1 else 1
    x_in, x_out, nsteps = _stream_specs(tt, bt, d, nt, batch_major_in)
    kern = functools.partial(_lru_kernel, tt=tt, bt=bt, nt=nt, batch_major_in=batch_major_in)
    row = lambda a: a.reshape(1, -1)
    return pl.pallas_call(
        kern,
        grid=(nb, nsteps),
        in_specs=[
            x_in,
            _const_spec((1, d)),
            _const_spec((d // 2, 2 * c)),
            _const_spec((LRU_CONV_W, c)),
            _const_spec((1, c)),
            _const_spec(w_gate.shape),
            _const_spec((1, c)),
            _const_spec((1, c)),
            _const_spec((1, c)),
            _const_spec((c // 2, d)),
            pl.BlockSpec((LRU_CONV_W - 1, bt, c), lambda ib, it: (0, ib, 0)),
            pl.BlockSpec((bt, c), lambda ib, it: (ib, 0)),
        ],
        out_specs=[
            x_out,
            pl.BlockSpec((LRU_CONV_W - 1, bt, c), lambda ib, it: (0, ib, 0)),
            pl.BlockSpec((bt, c), lambda ib, it: (ib, 0)),
        ],
        out_shape=[
            jax.ShapeDtypeStruct((t, b, d), F32),
            jax.ShapeDtypeStruct((LRU_CONV_W - 1, b, c), F32),
            jax.ShapeDtypeStruct((b, c), F32),
        ],
        scratch_shapes=[
            pltpu.VMEM((p + nblk * r, c), F32),
            pltpu.VMEM((nblk * r, c), F32),
            pltpu.VMEM((nblk * r, d), F32),
            pltpu.VMEM((r, d), BF16),
            pltpu.VMEM((r, c), BF16),
            pltpu.VMEM((r, d), F32),
            pltpu.VMEM((bt, c), F32),
        ],
        compiler_params=_compiler_params(),
        name="lru_layer",
    )(x, row(g), w_in, conv_w, row(conv_b), w_gate, row(b_a), row(b_i), row(lam), w_out, conv0, h0)


def _ccm_layer(x, conv0, g, w_in, b_in, dw_w, dw_b, ln_g, ln_b, w_out, b_out, final_g, *,
               tt, bt, final_norm, batch_major_out=False):
    t, b, d = x.shape
    c = dw_w.shape[-1]
    assert w_in.shape == (d // 2, 3 * c) and w_out.shape == (c // 2, d)
    nt = t // tt
    nb = b // bt
    r = tt * bt
    p = (CCM_CONV_W - 1) * bt
    tg = min(tt, CONV_ACC_VREGS * SUBLANES // bt)
    assert t % tt == 0 and b % bt == 0 and bt % SUBLANES == 0
    assert tt % tg == 0 and r % NORM_ROW_TILE == 0 and c % MXU_COLS == 0
    assert nt == 1 or r >= p
    assert r % min(r, max(r // 2, MIN_OUT_ROWS)) == 0
    kern = functools.partial(_ccm_kernel, tt=tt, bt=bt, nt=nt, final_norm=final_norm,
                             batch_major_out=batch_major_out)
    if batch_major_out:
        xo_spec = pl.BlockSpec((bt, tt, d), lambda ib, it: (ib, it, 0))
        xo_shape = jax.ShapeDtypeStruct((b, t, d), F32)
    else:
        xo_spec = pl.BlockSpec((tt, bt, d), lambda ib, it: (it, ib, 0))
        xo_shape = jax.ShapeDtypeStruct((t, b, d), F32)
    row = lambda a: a.reshape(1, -1)
    x_spec = pl.BlockSpec((tt, bt, d), lambda ib, it: (it, ib, 0))
    return pl.pallas_call(
        kern,
        grid=(nb, nt),
        in_specs=[
            x_spec,
            _const_spec((1, d)),
            _const_spec((d // 2, 3 * c)),
            _const_spec((1, 3 * c)),
            _const_spec((CCM_CONV_W, c)),
            _const_spec((1, c)),
            _const_spec((1, c)),
            _const_spec((1, c)),
            _const_spec((c // 2, d)),
            _const_spec((1, d)),
            _const_spec((1, d)),
            pl.BlockSpec((CCM_CONV_W - 1, bt, c), lambda ib, it: (0, ib, 0)),
        ],
        out_specs=[
            xo_spec,
            pl.BlockSpec((CCM_CONV_W - 1, bt, c), lambda ib, it: (0, ib, 0)),
        ],
        out_shape=[
            xo_shape,
            jax.ShapeDtypeStruct((CCM_CONV_W - 1, b, c), F32),
        ],
        scratch_shapes=[
            pltpu.VMEM((p + r, c), F32),
            pltpu.VMEM((r, c), F32),
            pltpu.VMEM((r, c), F32),
            pltpu.VMEM((r, c), BF16),
            pltpu.VMEM((CCM_CONV_W, SUBLANES, c), F32),
        ],
        compiler_params=_compiler_params(),
        name="ccm_layer",
    )(x, row(g), w_in, row(b_in), dw_w, row(dw_b), row(ln_g), row(ln_b), w_out, row(b_out),
      row(final_g), conv0)


def kernel(x_prompt, x_sample, state_lru_conv, state_lru_h, state_ccm_conv, norm_g, final_norm_g,
           lru_w_in, lru_conv_w, lru_conv_b, lru_w_a, lru_b_a, lru_w_i, lru_b_i, lru_lam, lru_w_out,
           ccm_w_in, ccm_b_in, ccm_dw_w, ccm_dw_b, ccm_ln_g, ccm_ln_b, ccm_w_out, ccm_b_out):
    depth = norm_g.shape[0]
    bp = x_prompt.shape[0]
    c = lru_w_out.shape[1]
    assert depth % 2 == 0
    tm = lambda a: jnp.swapaxes(a, -3, -2)

    xp = x_prompt
    xs = tm(x_sample)
    s_lru_conv = tm(state_lru_conv)
    s_ccm_conv = tm(state_ccm_conv)
    z_lru_conv = jnp.zeros((LRU_CONV_W - 1, bp, c), F32)
    z_lru_h = jnp.zeros((bp, c), F32)
    z_ccm_conv = jnp.zeros((CCM_CONV_W - 1, bp, c), F32)

    lru_w_in_b = _pack_rows(lru_w_in)
    lru_w_out_b = _pack_rows(lru_w_out)
    lru_w_gate_b = _pack_rows(jnp.concatenate([lru_w_a, lru_w_i], axis=-1))
    ccm_w_in_b = _pack_rows(ccm_w_in)
    ccm_w_out_b = _pack_rows(ccm_w_out)

    lru_conv_p, lru_h_p, ccm_conv_p = [], [], []
    lru_conv_s, lru_h_s, ccm_conv_s = [], [], []
    for l in range(depth):
        j = l // 2
        if l % 2 == 0:
            lw = (norm_g[l], lru_w_in_b[j], lru_conv_w[j], lru_conv_b[j], lru_w_gate_b[j],
                  lru_b_a[j], lru_b_i[j], lru_lam[j], lru_w_out_b[j])
            xp, cb, hl = _lru_layer(xp, z_lru_conv, z_lru_h, *lw, tt=32, bt=bp, batch_major_in=l == 0)
            lru_conv_p.append(cb)
            lru_h_p.append(hl)
            xs, cb, hl = _lru_layer(xs, s_lru_conv[j], state_lru_h[j], *lw, tt=xs.shape[0], bt=xs.shape[1])
            lru_conv_s.append(cb)
            lru_h_s.append(hl)
        else:
            last = l == depth - 1
            cw = (norm_g[l], ccm_w_in_b[j], ccm_b_in[j], ccm_dw_w[j], ccm_dw_b[j], ccm_ln_g[j],
                  ccm_ln_b[j], ccm_w_out_b[j], ccm_b_out[j], final_norm_g)
            xp, cb = _ccm_layer(xp, z_ccm_conv, *cw, tt=64, bt=bp, final_norm=last, batch_major_out=last)
            ccm_conv_p.append(cb)
            xs, cb = _ccm_layer(xs, s_ccm_conv[j], *cw, tt=xs.shape[0], bt=16, final_norm=last)
            ccm_conv_s.append(cb)
    return (xp, tm(xs),
            tm(jnp.stack(lru_conv_p)), jnp.stack(lru_h_p), tm(jnp.stack(ccm_conv_p)),
            tm(jnp.stack(lru_conv_s)), jnp.stack(lru_h_s), tm(jnp.stack(ccm_conv_s)))
```

```python
import functools

import jax
import jax.numpy as jnp
from jax import lax
from jax.experimental import pallas as pl
from jax.experimental.pallas import tpu as pltpu

F32 = jnp.float32
BF16 = jnp.bfloat16

RMS_EPS = 1e-6
LN_EPS = 1e-5
LRU_C = 8.0
LRU_BLOCK = 128
LRU_CONV_W = 4
CCM_CONV_W = 31

V7X_VMEM_BYTES = 64 * 1024 * 1024
SUBLANES = 8
LANES = 128
VMEM_LIMIT_BYTES = V7X_VMEM_BYTES - 8 * 1024 * 1024

NORM_ROW_TILE = 32
CONV_ACC_VREGS = 16
MXU_COLS = 256
MIN_OUT_ROWS = 128
PACK_BLOCK_ROWS = 1024
PACK_BLOCK_COLS = 1024


def _rmsnorm(x, g):
    ms = jnp.mean(x * x, axis=-1, keepdims=True)
    return (x * lax.rsqrt(ms + RMS_EPS)) * g


def _silu(x):
    return x * jax.nn.sigmoid(x)


def _bf16_bits(x):
    u = pltpu.bitcast(x, jnp.uint32)
    return (u + jnp.uint32(0x7FFF) + ((u >> 16) & jnp.uint32(1))) >> 16


def _pack_kernel(w_ref, o_ref, stage_s):
    kb = w_ref.shape[0]
    for j in range(w_ref.shape[1] // LANES):
        cols = slice(j * LANES, (j + 1) * LANES)
        stage_s[j] = w_ref[:, cols]
        lo = _bf16_bits(stage_s[j, pl.ds(0, kb // 2, stride=2), :])
        hi = _bf16_bits(stage_s[j, pl.ds(1, kb // 2, stride=2), :])
        o_ref[:, cols] = lo | (hi << 16)


def _pack_rows(w):
    *lead, k, n = w.shape
    assert k % 2 == 0
    w2 = w.reshape(-1, n)
    rows = w2.shape[0]
    kb, nb = min(rows, PACK_BLOCK_ROWS), min(n, PACK_BLOCK_COLS)
    assert rows % kb == 0 and n % nb == 0 and kb % (2 * SUBLANES) == 0 and nb % LANES == 0
    packed = pl.pallas_call(
        _pack_kernel,
        grid=(rows // kb, n // nb),
        in_specs=[pl.BlockSpec((kb, nb), lambda a, b: (a, b))],
        out_specs=pl.BlockSpec((kb // 2, nb), lambda a, b: (a, b)),
        out_shape=jax.ShapeDtypeStruct((rows // 2, n), jnp.uint32),
        scratch_shapes=[pltpu.VMEM((nb // LANES, kb, LANES), F32)],
        name="pack_weights",
    )(w2)
    return packed.reshape(*lead, k // 2, n)


def _rows_bf16(packed):
    return pltpu.bitcast(packed, BF16)


def _interleave(front, back):
    merged, i = [], 0
    for j, item in enumerate(back):
        while i < len(front) and i * len(back) <= j * len(front):
            merged.append(front[i])
            i += 1
        merged.append(item)
    merged.extend(front[i:])
    return merged


def _run_halves(front, back, pipelined):
    for part in (_interleave(front, back) if pipelined else front + back):
        part()


def _lru_kernel(x_ref, g_ref, win_ref, cw_ref, cb_ref, wg_ref, ba_ref, bi_ref,
                lam_ref, wout_ref, conv0_ref, h0_ref,
                xo_ref, convo_ref, ho_ref,
                xp_s, z_s, xr_s, hn_s, y_s, acc_s, h_s, *, tt, bt, nt, batch_major_in):
    d = x_ref.shape[-1]
    c = xp_s.shape[-1]
    r = tt * bt
    p = (LRU_CONV_W - 1) * bt
    pipelined = nt > 1
    nsteps = nt + 1 if pipelined else nt
    xb_off, xa_off = (p, p + r) if pipelined else (p, p)
    zb_off, za_off = (0, r) if pipelined else (0, 0)
    it = pl.program_id(1)

    @pl.when(it == 0)
    def _():
        if pipelined:
            xp_s[0:p + r, :] = jnp.zeros((p + r, c), F32)
            z_s[0:r, :] = jnp.zeros((r, c), F32)
            xr_s[0:r, :] = jnp.zeros((r, d), F32)
            h_s[...] = jnp.zeros(h_s.shape, F32)
        else:
            xp_s[0:p, :] = conv0_ref[...].reshape(p, c)
            h_s[...] = h0_ref[...]

    def front_norm():
        if batch_major_in:
            x = pltpu.einshape("btd->(tb)d", x_ref[...])
        else:
            x = x_ref[...].reshape(r, d)
        xr_s[za_off:za_off + r, :] = x
        hn_s[...] = _rmsnorm(x, g_ref[...]).astype(BF16)

    def front_x(j):
        cols = slice(j * MXU_COLS, (j + 1) * MXU_COLS)
        xp_s[xa_off:xa_off + r, cols] = jnp.dot(hn_s[...], _rows_bf16(win_ref[:, cols]), preferred_element_type=F32)

    def front_z(j):
        cols = slice(j * MXU_COLS, (j + 1) * MXU_COLS)
        wcols = slice(c + j * MXU_COLS, c + (j + 1) * MXU_COLS)
        z_s[za_off:za_off + r, cols] = _silu(jnp.dot(hn_s[...], _rows_bf16(win_ref[:, wcols]), preferred_element_type=F32))

    def back_head(h):
        cols = slice(h * LRU_BLOCK, (h + 1) * LRU_BLOCK)
        base = xb_off - p
        xc = cb_ref[:, cols] + cw_ref[0:1, cols] * xp_s[base:base + r, cols]
        for k in range(1, LRU_CONV_W):
            xc = xc + cw_ref[k:k + 1, cols] * xp_s[base + k * bt:base + k * bt + r, cols]
        gates = jnp.dot(xc.astype(BF16), _rows_bf16(wg_ref[h]), preferred_element_type=F32)
        rg = jax.nn.sigmoid(gates[:, 0:LRU_BLOCK] + ba_ref[:, cols])
        ig = jax.nn.sigmoid(gates[:, LRU_BLOCK:] + bi_ref[:, cols])
        log_a = (-LRU_C * rg) * jax.nn.softplus(-lam_ref[:, cols])
        a = jnp.exp(log_a)
        b = jnp.sqrt(-jnp.tanh(log_a) * (a * a + 1.0)) * (ig * xc)
        hcur = h_s[:, cols]
        hs = []
        for t in range(tt):
            hcur = a[t * bt:(t + 1) * bt] * hcur + b[t * bt:(t + 1) * bt]
            hs.append(hcur)
        h_s[:, cols] = hcur
        hall = jnp.concatenate(hs, axis=0)
        y_s[:, cols] = (hall * z_s[zb_off:zb_off + r, cols]).astype(BF16)

    def back_out(j):
        rows = slice(j * MXU_COLS, (j + 1) * MXU_COLS)
        wrows = slice(j * MXU_COLS // 2, (j + 1) * MXU_COLS // 2)
        part = jnp.dot(y_s[:, rows], _rows_bf16(wout_ref[wrows, :]), preferred_element_type=F32)
        if j == 0:
            acc_s[...] = xr_s[zb_off:zb_off + r, :] + part
        elif j < c // MXU_COLS - 1:
            acc_s[...] = acc_s[...] + part
        else:
            xo_ref[...] = (acc_s[...] + part).reshape(tt, bt, d)

    heads_per_chunk = MXU_COLS // LRU_BLOCK
    front = [front_norm]
    for j in range(c // MXU_COLS):
        front += [functools.partial(front_x, j), functools.partial(front_z, j)]
    back = []
    for j in range(c // MXU_COLS):
        back += [functools.partial(back_head, j * heads_per_chunk + i) for i in range(heads_per_chunk)]
        back.append(functools.partial(back_out, j))
    _run_halves(front, back, pipelined)

    if pipelined:
        xp_s[0:p + r, :] = xp_s[r:p + 2 * r, :]
        z_s[0:r, :] = z_s[r:2 * r, :]
        xr_s[0:r, :] = xr_s[r:2 * r, :]

        @pl.when(it == 0)
        def _():
            xp_s[0:p, :] = conv0_ref[...].reshape(p, c)
            h_s[...] = h0_ref[...]

    @pl.when(it == nsteps - 1)
    def _():
        tail = 0 if pipelined else r
        convo_ref[...] = xp_s[tail:tail + p, :].reshape(LRU_CONV_W - 1, bt, c)
        ho_ref[...] = h_s[...]


def _ccm_kernel(x_ref, g_ref, win_ref, bin_ref, dw_ref, dwb_ref, lng_ref, lnb_ref,
                wout_ref, bout_ref, fg_ref, conv0_ref,
                xo_ref, convo_ref,
                cp_s, z_s, y_s, s_s, wb_s, *, tt, bt, nt, final_norm, batch_major_out):
    d = x_ref.shape[-1]
    c = z_s.shape[-1]
    r = tt * bt
    p = (CCM_CONV_W - 1) * bt
    it = pl.program_id(1)

    @pl.when(it == 0)
    def _():
        for k in range(CCM_CONV_W):
            wb_s[k] = jnp.broadcast_to(dw_ref[k:k + 1, :], (SUBLANES, c))
        for m in range((CCM_CONV_W + 1) // 2):
            pair = dw_ref[2 * m:2 * m + 1, :]
            if 2 * m + 1 < CCM_CONV_W:
                pair = pair + dw_ref[2 * m + 1:2 * m + 2, :]
            wb_s[CCM_CONV_W + m] = jnp.broadcast_to(pair, (SUBLANES, c))
        cp_s[0:p, :] = conv0_ref[...].reshape(p, c)
        cp_s[p + r:p + r + bt, :] = jnp.zeros((bt, c), F32)

    def lane_chunk(start, width):
        return pl.ds(pl.multiple_of(start, width), width)

    n_pairs = tt // 2
    gp = min(n_pairs, CONV_ACC_VREGS * SUBLANES // bt)
    n_e = (CCM_CONV_W + 1) // 2
    n_o = CCM_CONV_W // 2

    def conv_chunk(jc, carry):
        cols = lane_chunk(jc * LANES, LANES)

        def step(u):
            return cp_s[u * bt:(u + 1) * bt, cols].reshape(bt // SUBLANES, SUBLANES, LANES)

        def half_filter(taps, inputs, v0, accs):
            accs = list(accs)
            for i in range(len(accs) + len(taps) - 1):
                xi = inputs(v0 + i)
                for v in range(max(0, i - len(taps) + 1), min(len(accs), i + 1)):
                    term = xi * taps[i - v][None]
                    accs[v] = term if accs[v] is None else accs[v] + term
            return accs

        bias = jnp.broadcast_to(dwb_ref[:, cols], (bt, LANES)).reshape(bt // SUBLANES, SUBLANES, LANES)
        for v0 in range(0, n_pairs, gp):
            he = [wb_s[2 * m, :, cols] for m in range(n_e)]
            a = half_filter(he, lambda i: step(2 * i), v0, [None] * (gp + 1))
            ho = [wb_s[2 * m + 1, :, cols] for m in range(n_o)]
            b = half_filter(ho, lambda i: step(2 * i + 1), v0, [None] * gp)
            odd0 = []
            for v in range(gp):
                t0 = 2 * (v0 + v)
                y_s[t0 * bt:(t0 + 1) * bt, cols] = (bias + (a[v] + b[v])).reshape(bt, LANES)
                odd0.append(bias - (a[v + 1] + b[v]))
            hs = [wb_s[CCM_CONV_W + m, :, cols] for m in range(n_e)]
            odd = half_filter(hs, lambda i: step(2 * i + 1) + step(2 * i + 2), v0, odd0)
            for v in range(gp):
                t1 = 2 * (v0 + v) + 1
                y_s[t1 * bt:(t1 + 1) * bt, cols] = odd[v].reshape(bt, LANES)
        return carry

    rt = NORM_ROW_TILE

    def norm_tile(i):
        r0 = i * rt
        y = y_s[r0:r0 + rt, :]
        mu = jnp.mean(y, axis=-1, keepdims=True)
        yc = y - mu
        var = jnp.mean(yc * yc, axis=-1, keepdims=True)
        yn = yc * lax.rsqrt(var + LN_EPS) * lng_ref[...] + lnb_ref[...]
        s_s[r0:r0 + rt, :] = (_silu(yn) * z_s[r0:r0 + rt, :]).astype(BF16)

    def out_rows(r0, nrows):
        out = jnp.dot(s_s[r0:r0 + nrows, :], _rows_bf16(wout_ref[...]), preferred_element_type=F32) + bout_ref[...]
        xn = x_ref[r0 // bt:(r0 + nrows) // bt].reshape(nrows, d) + out
        if final_norm:
            xn = _rmsnorm(xn, fg_ref[...])
        if batch_major_out:
            xo_ref[:, r0 // bt:(r0 + nrows) // bt, :] = pltpu.einshape("(tb)d->btd", xn, b=bt)
        else:
            xo_ref[r0 // bt:(r0 + nrows) // bt] = xn.reshape(nrows // bt, bt, d)

    hn = _rmsnorm(x_ref[...].reshape(r, d), g_ref[...]).astype(BF16)

    def in_proj(part):
        wcols = slice(part * c, (part + 1) * c)
        return jnp.dot(hn, _rows_bf16(win_ref[:, wcols]), preferred_element_type=F32) + bin_ref[:, wcols]

    cp_s[p:p + r, :] = in_proj(0) * jax.nn.sigmoid(in_proj(1))
    z_s[...] = _silu(in_proj(2))
    lax.fori_loop(0, c // LANES, conv_chunk, 0)
    group = min(r, max(r // 2, MIN_OUT_ROWS))
    for i in range(r // rt):
        norm_tile(i)
        if ((i + 1) * rt) % group == 0:
            out_rows((i + 1) * rt - group, group)

    @pl.when(it == nt - 1)
    def _():
        convo_ref[...] = cp_s[r:r + p, :].reshape(CCM_CONV_W - 1, bt, c)

    if nt > 1:
        cp_s[0:p, :] = cp_s[r:r + p, :]


def _const_spec(shape):
    nd = len(shape)
    return pl.BlockSpec(shape, lambda ib, it, nd=nd: (0,) * nd, pipeline_mode=pl.Buffered(1))


def _stream_specs(tt, bt, d, nt, batch_major_in):
    if nt > 1:
        in_map = lambda ib, it: (jnp.minimum(it, nt - 1), ib, 0)
        out_map = lambda ib, it: (jnp.maximum(it - 1, 0), ib, 0)
        nsteps = nt + 1
    else:
        in_map = out_map = lambda ib, it: (it, ib, 0)
        nsteps = nt
    if batch_major_in:
        x_in = pl.BlockSpec((bt, tt, d), lambda ib, it: tuple(in_map(ib, it)[i] for i in (1, 0, 2)))
    else:
        x_in = pl.BlockSpec((tt, bt, d), in_map)
    return x_in, pl.BlockSpec((tt, bt, d), out_map), nsteps


def _compiler_params():
    return pltpu.CompilerParams(
        dimension_semantics=("arbitrary", "arbitrary"),
        vmem_limit_bytes=VMEM_LIMIT_BYTES)


def _lru_layer(x, conv0, h0, g, w_in, conv_w, conv_b, w_gate, b_a, b_i, lam, w_out, *, tt, bt,
               batch_major_in=False):
    (b, t, d) = x.shape if batch_major_in else (x.shape[1], x.shape[0], x.shape[2])
    c = conv_w.shape[-1]
    assert w_in.shape == (d // 2, 2 * c) and w_out.shape == (c // 2, d)
    nt = t // tt
    nb = b // bt
    r = tt * bt
    p = (LRU_CONV_W - 1) * bt
    assert t % tt == 0 and b % bt == 0 and bt % SUBLANES == 0 and r % (2 * SUBLANES) == 0
    assert nt == 1 or r >= p
    assert c % MXU_COLS == 0 and c // MXU_COLS >= 2 and MXU_COLS % LRU_BLOCK == 0
    nblk = 2 if nt > 1 else 1
    x_in, x_out, nsteps = _stream_specs(tt, bt, d, nt, batch_major_in)
    kern = functools.partial(_lru_kernel, tt=tt, bt=bt, nt=nt, batch_major_in=batch_major_in)
    row = lambda a: a.reshape(1, -1)
    return pl.pallas_call(
        kern,
        grid=(nb, nsteps),
        in_specs=[
            x_in,
            _const_spec((1, d)),
            _const_spec((d // 2, 2 * c)),
            _const_spec((LRU_CONV_W, c)),
            _const_spec((1, c)),
            _const_spec(w_gate.shape),
            _const_spec((1, c)),
            _const_spec((1, c)),
            _const_spec((1, c)),
            _const_spec((c // 2, d)),
            pl.BlockSpec((LRU_CONV_W - 1, bt, c), lambda ib, it: (0, ib, 0)),
            pl.BlockSpec((bt, c), lambda ib, it: (ib, 0)),
        ],
        out_specs=[
            x_out,
            pl.BlockSpec((LRU_CONV_W - 1, bt, c), lambda ib, it: (0, ib, 0)),
            pl.BlockSpec((bt, c), lambda ib, it: (ib, 0)),
        ],
        out_shape=[
            jax.ShapeDtypeStruct((t, b, d), F32),
            jax.ShapeDtypeStruct((LRU_CONV_W - 1, b, c), F32),
            jax.ShapeDtypeStruct((b, c), F32),
        ],
        scratch_shapes=[
            pltpu.VMEM((p + nblk * r, c), F32),
            pltpu.VMEM((nblk * r, c), F32),
            pltpu.VMEM((nblk * r, d), F32),
            pltpu.VMEM((r, d), BF16),
            pltpu.VMEM((r, c), BF16),
            pltpu.VMEM((r, d), F32),
            pltpu.VMEM((bt, c), F32),
        ],
        compiler_params=_compiler_params(),
        name="lru_layer",
    )(x, row(g), w_in, conv_w, row(conv_b), w_gate, row(b_a), row(b_i), row(lam), w_out, conv0, h0)


def _ccm_layer(x, conv0, g, w_in, b_in, dw_w, dw_b, ln_g, ln_b, w_out, b_out, final_g, *,
               tt, bt, final_norm, batch_major_out=False):
    t, b, d = x.shape
    c = dw_w.shape[-1]
    assert w_in.shape == (d // 2, 3 * c) and w_out.shape == (c // 2, d)
    nt = t // tt
    nb = b // bt
    r = tt * bt
    p = (CCM_CONV_W - 1) * bt
    gp = min(tt // 2, CONV_ACC_VREGS * SUBLANES // bt)
    assert t % tt == 0 and b % bt == 0 and bt % SUBLANES == 0
    assert tt % 2 == 0 and (tt // 2) % gp == 0 and r % NORM_ROW_TILE == 0 and c % MXU_COLS == 0
    assert nt == 1 or r >= p
    assert r % min(r, max(r // 2, MIN_OUT_ROWS)) == 0
    kern = functools.partial(_ccm_kernel, tt=tt, bt=bt, nt=nt, final_norm=final_norm,
                             batch_major_out=batch_major_out)
    if batch_major_out:
        xo_spec = pl.BlockSpec((bt, tt, d), lambda ib, it: (ib, it, 0))
        xo_shape = jax.ShapeDtypeStruct((b, t, d), F32)
    else:
        xo_spec = pl.BlockSpec((tt, bt, d), lambda ib, it: (it, ib, 0))
        xo_shape = jax.ShapeDtypeStruct((t, b, d), F32)
    row = lambda a: a.reshape(1, -1)
    x_spec = pl.BlockSpec((tt, bt, d), lambda ib, it: (it, ib, 0))
    return pl.pallas_call(
        kern,
        grid=(nb, nt),
        in_specs=[
            x_spec,
            _const_spec((1, d)),
            _const_spec((d // 2, 3 * c)),
            _const_spec((1, 3 * c)),
            _const_spec((CCM_CONV_W, c)),
            _const_spec((1, c)),
            _const_spec((1, c)),
            _const_spec((1, c)),
            _const_spec((c // 2, d)),
            _const_spec((1, d)),
            _const_spec((1, d)),
            pl.BlockSpec((CCM_CONV_W - 1, bt, c), lambda ib, it: (0, ib, 0)),
        ],
        out_specs=[
            xo_spec,
            pl.BlockSpec((CCM_CONV_W - 1, bt, c), lambda ib, it: (0, ib, 0)),
        ],
        out_shape=[
            xo_shape,
            jax.ShapeDtypeStruct((CCM_CONV_W - 1, b, c), F32),
        ],
        scratch_shapes=[
            pltpu.VMEM((p + r + bt, c), F32),
            pltpu.VMEM((r, c), F32),
            pltpu.VMEM((r, c), F32),
            pltpu.VMEM((r, c), BF16),
            pltpu.VMEM((CCM_CONV_W + (CCM_CONV_W + 1) // 2, SUBLANES, c), F32),
        ],
        compiler_params=_compiler_params(),
        name="ccm_layer",
    )(x, row(g), w_in, row(b_in), dw_w, row(dw_b), row(ln_g), row(ln_b), w_out, row(b_out),
      row(final_g), conv0)


def kernel(x_prompt, x_sample, state_lru_conv, state_lru_h, state_ccm_conv, norm_g, final_norm_g,
           lru_w_in, lru_conv_w, lru_conv_b, lru_w_a, lru_b_a, lru_w_i, lru_b_i, lru_lam, lru_w_out,
           ccm_w_in, ccm_b_in, ccm_dw_w, ccm_dw_b, ccm_ln_g, ccm_ln_b, ccm_w_out, ccm_b_out):
    depth = norm_g.shape[0]
    bp = x_prompt.shape[0]
    c = lru_w_out.shape[1]
    assert depth % 2 == 0
    tm = lambda a: jnp.swapaxes(a, -3, -2)

    xp = x_prompt
    xs = tm(x_sample)
    s_lru_conv = tm(state_lru_conv)
    s_ccm_conv = tm(state_ccm_conv)
    z_lru_conv = jnp.zeros((LRU_CONV_W - 1, bp, c), F32)
    z_lru_h = jnp.zeros((bp, c), F32)
    z_ccm_conv = jnp.zeros((CCM_CONV_W - 1, bp, c), F32)

    lru_w_in_b = _pack_rows(lru_w_in)
    lru_w_out_b = _pack_rows(lru_w_out)
    lru_w_gate_b = _pack_rows(jnp.concatenate([lru_w_a, lru_w_i], axis=-1))
    ccm_w_in_b = _pack_rows(ccm_w_in)
    ccm_w_out_b = _pack_rows(ccm_w_out)

    lru_conv_p, lru_h_p, ccm_conv_p = [], [], []
    lru_conv_s, lru_h_s, ccm_conv_s = [], [], []
    for l in range(depth):
        j = l // 2
        if l % 2 == 0:
            lw = (norm_g[l], lru_w_in_b[j], lru_conv_w[j], lru_conv_b[j], lru_w_gate_b[j],
                  lru_b_a[j], lru_b_i[j], lru_lam[j], lru_w_out_b[j])
            xp, cb, hl = _lru_layer(xp, z_lru_conv, z_lru_h, *lw, tt=32, bt=bp, batch_major_in=l == 0)
            lru_conv_p.append(cb)
            lru_h_p.append(hl)
            xs, cb, hl = _lru_layer(xs, s_lru_conv[j], state_lru_h[j], *lw, tt=xs.shape[0], bt=xs.shape[1])
            lru_conv_s.append(cb)
            lru_h_s.append(hl)
        else:
            last = l == depth - 1
            cw = (norm_g[l], ccm_w_in_b[j], ccm_b_in[j], ccm_dw_w[j], ccm_dw_b[j], ccm_ln_g[j],
                  ccm_ln_b[j], ccm_w_out_b[j], ccm_b_out[j], final_norm_g)
            xp, cb = _ccm_layer(xp, z_ccm_conv, *cw, tt=64, bt=bp, final_norm=last, batch_major_out=last)
            ccm_conv_p.append(cb)
            xs, cb = _ccm_layer(xs, s_ccm_conv[j], *cw, tt=xs.shape[0], bt=16, final_norm=last)
            ccm_conv_s.append(cb)
    return (xp, tm(xs),
            tm(jnp.stack(lru_conv_p)), jnp.stack(lru_h_p), tm(jnp.stack(ccm_conv_p)),
            tm(jnp.stack(lru_conv_s)), jnp.stack(lru_h_s), tm(jnp.stack(ccm_conv_s)))
```

```python
import functools

import jax
import jax.numpy as jnp
from jax import lax
from jax.experimental import pallas as pl
from jax.experimental.pallas import tpu as pltpu

F32 = jnp.float32
BF16 = jnp.bfloat16

RMS_EPS = 1e-6
LN_EPS = 1e-5
LRU_C = 8.0
LRU_BLOCK = 128
LRU_CONV_W = 4
CCM_CONV_W = 31

V7X_VMEM_BYTES = 64 * 1024 * 1024
SUBLANES = 8
LANES = 128
VMEM_LIMIT_BYTES = V7X_VMEM_BYTES - 8 * 1024 * 1024

NORM_ROW_TILE = 32
CONV_FIR_LEVELS = 2
CONV_GROUP_STEPS = 64
CONV_FIR_PAD = 2 ** CONV_FIR_LEVELS - 1
MXU_COLS = 256
MIN_OUT_ROWS = 128
PACK_BLOCK_ROWS = 1024
PACK_BLOCK_COLS = 1024


def _rmsnorm(x, g):
    ms = jnp.mean(x * x, axis=-1, keepdims=True)
    return (x * lax.rsqrt(ms + RMS_EPS)) * g


def _silu(x):
    return x * jax.nn.sigmoid(x)


def _bf16_bits(x):
    u = pltpu.bitcast(x, jnp.uint32)
    return (u + jnp.uint32(0x7FFF) + ((u >> 16) & jnp.uint32(1))) >> 16


def _pack_kernel(w_ref, o_ref, stage_s):
    kb = w_ref.shape[0]
    for j in range(w_ref.shape[1] // LANES):
        cols = slice(j * LANES, (j + 1) * LANES)
        stage_s[j] = w_ref[:, cols]
        lo = _bf16_bits(stage_s[j, pl.ds(0, kb // 2, stride=2), :])
        hi = _bf16_bits(stage_s[j, pl.ds(1, kb // 2, stride=2), :])
        o_ref[:, cols] = lo | (hi << 16)


def _pack_rows(w):
    *lead, k, n = w.shape
    assert k % 2 == 0
    w2 = w.reshape(-1, n)
    rows = w2.shape[0]
    kb, nb = min(rows, PACK_BLOCK_ROWS), min(n, PACK_BLOCK_COLS)
    assert rows % kb == 0 and n % nb == 0 and kb % (2 * SUBLANES) == 0 and nb % LANES == 0
    packed = pl.pallas_call(
        _pack_kernel,
        grid=(rows // kb, n // nb),
        in_specs=[pl.BlockSpec((kb, nb), lambda a, b: (a, b))],
        out_specs=pl.BlockSpec((kb // 2, nb), lambda a, b: (a, b)),
        out_shape=jax.ShapeDtypeStruct((rows // 2, n), jnp.uint32),
        scratch_shapes=[pltpu.VMEM((nb // LANES, kb, LANES), F32)],
        name="pack_weights",
    )(w2)
    return packed.reshape(*lead, k // 2, n)


def _rows_bf16(packed):
    return pltpu.bitcast(packed, BF16)


def _interleave(front, back):
    merged, i = [], 0
    for j, item in enumerate(back):
        while i < len(front) and i * len(back) <= j * len(front):
            merged.append(front[i])
            i += 1
        merged.append(item)
    merged.extend(front[i:])
    return merged


def _run_halves(front, back, pipelined):
    for part in (_interleave(front, back) if pipelined else front + back):
        part()


def _lru_kernel(x_ref, g_ref, win_ref, cw_ref, cb_ref, wg_ref, ba_ref, bi_ref,
                lam_ref, wout_ref, conv0_ref, h0_ref,
                xo_ref, convo_ref, ho_ref,
                xp_s, z_s, xr_s, hn_s, y_s, acc_s, h_s, *, tt, bt, nt, batch_major_in):
    d = x_ref.shape[-1]
    c = xp_s.shape[-1]
    r = tt * bt
    p = (LRU_CONV_W - 1) * bt
    pipelined = nt > 1
    nsteps = nt + 1 if pipelined else nt
    xb_off, xa_off = (p, p + r) if pipelined else (p, p)
    zb_off, za_off = (0, r) if pipelined else (0, 0)
    it = pl.program_id(1)

    @pl.when(it == 0)
    def _():
        if pipelined:
            xp_s[0:p + r, :] = jnp.zeros((p + r, c), F32)
            z_s[0:r, :] = jnp.zeros((r, c), F32)
            xr_s[0:r, :] = jnp.zeros((r, d), F32)
            h_s[...] = jnp.zeros(h_s.shape, F32)
        else:
            xp_s[0:p, :] = conv0_ref[...].reshape(p, c)
            h_s[...] = h0_ref[...]

    def front_norm():
        if batch_major_in:
            x = pltpu.einshape("btd->(tb)d", x_ref[...])
        else:
            x = x_ref[...].reshape(r, d)
        xr_s[za_off:za_off + r, :] = x
        hn_s[...] = _rmsnorm(x, g_ref[...]).astype(BF16)

    def front_x(j):
        cols = slice(j * MXU_COLS, (j + 1) * MXU_COLS)
        xp_s[xa_off:xa_off + r, cols] = jnp.dot(hn_s[...], _rows_bf16(win_ref[:, cols]), preferred_element_type=F32)

    def front_z(j):
        cols = slice(j * MXU_COLS, (j + 1) * MXU_COLS)
        wcols = slice(c + j * MXU_COLS, c + (j + 1) * MXU_COLS)
        z_s[za_off:za_off + r, cols] = _silu(jnp.dot(hn_s[...], _rows_bf16(win_ref[:, wcols]), preferred_element_type=F32))

    def back_head(h):
        cols = slice(h * LRU_BLOCK, (h + 1) * LRU_BLOCK)
        base = xb_off - p
        xc = cb_ref[:, cols] + cw_ref[0:1, cols] * xp_s[base:base + r, cols]
        for k in range(1, LRU_CONV_W):
            xc = xc + cw_ref[k:k + 1, cols] * xp_s[base + k * bt:base + k * bt + r, cols]
        gates = jnp.dot(xc.astype(BF16), _rows_bf16(wg_ref[h]), preferred_element_type=F32)
        rg = jax.nn.sigmoid(gates[:, 0:LRU_BLOCK] + ba_ref[:, cols])
        ig = jax.nn.sigmoid(gates[:, LRU_BLOCK:] + bi_ref[:, cols])
        log_a = (-LRU_C * rg) * jax.nn.softplus(-lam_ref[:, cols])
        a = jnp.exp(log_a)
        b = jnp.sqrt(-jnp.tanh(log_a) * (a * a + 1.0)) * (ig * xc)
        hcur = h_s[:, cols]
        hs = []
        for t in range(tt):
            hcur = a[t * bt:(t + 1) * bt] * hcur + b[t * bt:(t + 1) * bt]
            hs.append(hcur)
        h_s[:, cols] = hcur
        hall = jnp.concatenate(hs, axis=0)
        y_s[:, cols] = (hall * z_s[zb_off:zb_off + r, cols]).astype(BF16)

    def back_out(j):
        rows = slice(j * MXU_COLS, (j + 1) * MXU_COLS)
        wrows = slice(j * MXU_COLS // 2, (j + 1) * MXU_COLS // 2)
        part = jnp.dot(y_s[:, rows], _rows_bf16(wout_ref[wrows, :]), preferred_element_type=F32)
        if j == 0:
            acc_s[...] = xr_s[zb_off:zb_off + r, :] + part
        elif j < c // MXU_COLS - 1:
            acc_s[...] = acc_s[...] + part
        else:
            xo_ref[...] = (acc_s[...] + part).reshape(tt, bt, d)

    heads_per_chunk = MXU_COLS // LRU_BLOCK
    front = [front_norm]
    for j in range(c // MXU_COLS):
        front += [functools.partial(front_x, j), functools.partial(front_z, j)]
    back = []
    for j in range(c // MXU_COLS):
        back += [functools.partial(back_head, j * heads_per_chunk + i) for i in range(heads_per_chunk)]
        back.append(functools.partial(back_out, j))
    _run_halves(front, back, pipelined)

    if pipelined:
        xp_s[0:p + r, :] = xp_s[r:p + 2 * r, :]
        z_s[0:r, :] = z_s[r:2 * r, :]
        xr_s[0:r, :] = xr_s[r:2 * r, :]

        @pl.when(it == 0)
        def _():
            xp_s[0:p, :] = conv0_ref[...].reshape(p, c)
            h_s[...] = h0_ref[...]

    @pl.when(it == nsteps - 1)
    def _():
        tail = 0 if pipelined else r
        convo_ref[...] = xp_s[tail:tail + p, :].reshape(LRU_CONV_W - 1, bt, c)
        ho_ref[...] = h_s[...]


def _ccm_kernel(x_ref, g_ref, win_ref, bin_ref, dw_ref, dwb_ref, lng_ref, lnb_ref,
                wout_ref, bout_ref, fg_ref, conv0_ref,
                xo_ref, convo_ref,
                cp_s, z_s, y_s, s_s, wb_s, *, tt, bt, nt, final_norm, batch_major_out):
    d = x_ref.shape[-1]
    c = z_s.shape[-1]
    r = tt * bt
    p = (CCM_CONV_W - 1) * bt
    it = pl.program_id(1)

    @pl.when(it == 0)
    def _():
        for k in range(CCM_CONV_W):
            wb_s[k] = jnp.broadcast_to(dw_ref[k:k + 1, :], (SUBLANES, c))
        cp_s[0:p, :] = conv0_ref[...].reshape(p, c)
        cp_s[p + r:p + r + CONV_FIR_PAD * bt, :] = jnp.zeros((CONV_FIR_PAD * bt, c), F32)

    def lane_chunk(start, width):
        return pl.ds(pl.multiple_of(start, width), width)

    nsub = bt // SUBLANES
    go = min(tt, CONV_GROUP_STEPS)

    def tap_sum(ta, tb):
        return tb if ta is None else (ta if tb is None else ta + tb)

    def direct(taps, x, v0, n_out):
        accs = [None] * n_out
        for i in range(n_out + len(taps) - 1):
            xi = x(v0 + i)
            for v in range(max(0, i - len(taps) + 1), min(n_out, i + 1)):
                if taps[i - v] is not None:
                    term = xi * taps[i - v][None]
                    accs[v] = term if accs[v] is None else accs[v] + term
        return accs

    def split(taps):
        taps = list(taps) + [None] * (len(taps) % 2)
        te, to = taps[0::2], taps[1::2]
        return te, to, [tap_sum(ta, tb) for ta, tb in zip(te, to)]

    def corr(taps, x, v0, n_out, levels):
        if levels == 0 or n_out < 2:
            return direct(taps, x, v0, n_out)
        te, to, ts = split(taps)
        n2 = (n_out + 1) // 2
        a = corr(te, lambda i: x(2 * i), v0 // 2, n2 + 1, levels - 1)
        b = corr(to, lambda i: x(2 * i + 1), v0 // 2, n2, levels - 1)
        s = corr(ts, lambda i: x(2 * i + 1) + x(2 * i + 2), v0 // 2, n2, levels - 1)
        out = []
        for v in range(n2):
            out += [a[v] + b[v], s[v] - a[v + 1] - b[v]]
        return out[:n_out]

    def conv_chunk(jc, carry):
        cols = lane_chunk(jc * LANES, LANES)

        def step(u):
            return cp_s[u * bt:(u + 1) * bt, cols].reshape(nsub, SUBLANES, LANES)

        def put(t, val):
            y_s[t * bt:(t + 1) * bt, cols] = val.reshape(bt, LANES)

        bias = jnp.broadcast_to(dwb_ref[:, cols], (bt, LANES)).reshape(nsub, SUBLANES, LANES)
        te, to, ts = split([wb_s[k, :, cols] for k in range(CCM_CONV_W)])
        for v0 in range(0, tt, go):
            n2 = go // 2
            a = corr(te, lambda i: step(2 * i), v0 // 2, n2 + 1, CONV_FIR_LEVELS - 1)
            b = corr(to, lambda i: step(2 * i + 1), v0 // 2, n2, CONV_FIR_LEVELS - 1)
            odd0 = []
            for v in range(n2):
                put(v0 + 2 * v, bias + (a[v] + b[v]))
                odd0.append(bias - (a[v + 1] + b[v]))
            s = corr(ts, lambda i: step(2 * i + 1) + step(2 * i + 2), v0 // 2, n2, CONV_FIR_LEVELS - 1)
            for v in range(n2):
                put(v0 + 2 * v + 1, odd0[v] + s[v])
        return carry

    rt = NORM_ROW_TILE

    def norm_tile(i):
        r0 = i * rt
        y = y_s[r0:r0 + rt, :]
        mu = jnp.mean(y, axis=-1, keepdims=True)
        yc = y - mu
        var = jnp.mean(yc * yc, axis=-1, keepdims=True)
        yn = yc * lax.rsqrt(var + LN_EPS) * lng_ref[...] + lnb_ref[...]
        s_s[r0:r0 + rt, :] = (_silu(yn) * z_s[r0:r0 + rt, :]).astype(BF16)

    def out_rows(r0, nrows):
        out = jnp.dot(s_s[r0:r0 + nrows, :], _rows_bf16(wout_ref[...]), preferred_element_type=F32) + bout_ref[...]
        xn = x_ref[r0 // bt:(r0 + nrows) // bt].reshape(nrows, d) + out
        if final_norm:
            xn = _rmsnorm(xn, fg_ref[...])
        if batch_major_out:
            xo_ref[:, r0 // bt:(r0 + nrows) // bt, :] = pltpu.einshape("(tb)d->btd", xn, b=bt)
        else:
            xo_ref[r0 // bt:(r0 + nrows) // bt] = xn.reshape(nrows // bt, bt, d)

    hn = _rmsnorm(x_ref[...].reshape(r, d), g_ref[...]).astype(BF16)

    def in_proj(part):
        wcols = slice(part * c, (part + 1) * c)
        return jnp.dot(hn, _rows_bf16(win_ref[:, wcols]), preferred_element_type=F32) + bin_ref[:, wcols]

    cp_s[p:p + r, :] = in_proj(0) * jax.nn.sigmoid(in_proj(1))
    z_s[...] = _silu(in_proj(2))
    lax.fori_loop(0, c // LANES, conv_chunk, 0)
    group = min(r, max(r // 2, MIN_OUT_ROWS))
    for i in range(r // rt):
        norm_tile(i)
        if ((i + 1) * rt) % group == 0:
            out_rows((i + 1) * rt - group, group)

    @pl.when(it == nt - 1)
    def _():
        convo_ref[...] = cp_s[r:r + p, :].reshape(CCM_CONV_W - 1, bt, c)

    if nt > 1:
        cp_s[0:p, :] = cp_s[r:r + p, :]


def _const_spec(shape):
    nd = len(shape)
    return pl.BlockSpec(shape, lambda ib, it, nd=nd: (0,) * nd, pipeline_mode=pl.Buffered(1))


def _stream_specs(tt, bt, d, nt, batch_major_in):
    if nt > 1:
        in_map = lambda ib, it: (jnp.minimum(it, nt - 1), ib, 0)
        out_map = lambda ib, it: (jnp.maximum(it - 1, 0), ib, 0)
        nsteps = nt + 1
    else:
        in_map = out_map = lambda ib, it: (it, ib, 0)
        nsteps = nt
    if batch_major_in:
        x_in = pl.BlockSpec((bt, tt, d), lambda ib, it: tuple(in_map(ib, it)[i] for i in (1, 0, 2)))
    else:
        x_in = pl.BlockSpec((tt, bt, d), in_map)
    return x_in, pl.BlockSpec((tt, bt, d), out_map), nsteps


def _compiler_params():
    return pltpu.CompilerParams(
        dimension_semantics=("arbitrary", "arbitrary"),
        vmem_limit_bytes=VMEM_LIMIT_BYTES)


def _lru_layer(x, conv0, h0, g, w_in, conv_w, conv_b, w_gate, b_a, b_i, lam, w_out, *, tt, bt,
               batch_major_in=False):
    (b, t, d) = x.shape if batch_major_in else (x.shape[1], x.shape[0], x.shape[2])
    c = conv_w.shape[-1]
    assert w_in.shape == (d // 2, 2 * c) and w_out.shape == (c // 2, d)
    nt = t // tt
    nb = b // bt
    r = tt * bt
    p = (LRU_CONV_W - 1) * bt
    assert t % tt == 0 and b % bt == 0 and bt % SUBLANES == 0 and r % (2 * SUBLANES) == 0
    assert nt == 1 or r >= p
    assert c % MXU_COLS == 0 and c // MXU_COLS >= 2 and MXU_COLS % LRU_BLOCK == 0
    nblk = 2 if nt > 1 else 1
    x_in, x_out, nsteps = _stream_specs(tt, bt, d, nt, batch_major_in)
    kern = functools.partial(_lru_kernel, tt=tt, bt=bt, nt=nt, batch_major_in=batch_major_in)
    row = lambda a: a.reshape(1, -1)
    return pl.pallas_call(
        kern,
        grid=(nb, nsteps),
        in_specs=[
            x_in,
            _const_spec((1, d)),
            _const_spec((d // 2, 2 * c)),
            _const_spec((LRU_CONV_W, c)),
            _const_spec((1, c)),
            _const_spec(w_gate.shape),
            _const_spec((1, c)),
            _const_spec((1, c)),
            _const_spec((1, c)),
            _const_spec((c // 2, d)),
            pl.BlockSpec((LRU_CONV_W - 1, bt, c), lambda ib, it: (0, ib, 0)),
            pl.BlockSpec((bt, c), lambda ib, it: (ib, 0)),
        ],
        out_specs=[
            x_out,
            pl.BlockSpec((LRU_CONV_W - 1, bt, c), lambda ib, it: (0, ib, 0)),
            pl.BlockSpec((bt, c), lambda ib, it: (ib, 0)),
        ],
        out_shape=[
            jax.ShapeDtypeStruct((t, b, d), F32),
            jax.ShapeDtypeStruct((LRU_CONV_W - 1, b, c), F32),
            jax.ShapeDtypeStruct((b, c), F32),
        ],
        scratch_shapes=[
            pltpu.VMEM((p + nblk * r, c), F32),
            pltpu.VMEM((nblk * r, c), F32),
            pltpu.VMEM((nblk * r, d), F32),
            pltpu.VMEM((r, d), BF16),
            pltpu.VMEM((r, c), BF16),
            pltpu.VMEM((r, d), F32),
            pltpu.VMEM((bt, c), F32),
        ],
        compiler_params=_compiler_params(),
        name="lru_layer",
    )(x, row(g), w_in, conv_w, row(conv_b), w_gate, row(b_a), row(b_i), row(lam), w_out, conv0, h0)


def _ccm_layer(x, conv0, g, w_in, b_in, dw_w, dw_b, ln_g, ln_b, w_out, b_out, final_g, *,
               tt, bt, final_norm, batch_major_out=False):
    t, b, d = x.shape
    c = dw_w.shape[-1]
    assert w_in.shape == (d // 2, 3 * c) and w_out.shape == (c // 2, d)
    nt = t // tt
    nb = b // bt
    r = tt * bt
    p = (CCM_CONV_W - 1) * bt
    go = min(tt, CONV_GROUP_STEPS)
    assert t % tt == 0 and b % bt == 0 and bt % SUBLANES == 0
    assert tt % go == 0 and go % 2 ** CONV_FIR_LEVELS == 0 and r % NORM_ROW_TILE == 0 and c % MXU_COLS == 0
    assert nt == 1 or r >= p
    assert r % min(r, max(r // 2, MIN_OUT_ROWS)) == 0
    kern = functools.partial(_ccm_kernel, tt=tt, bt=bt, nt=nt, final_norm=final_norm,
                             batch_major_out=batch_major_out)
    if batch_major_out:
        xo_spec = pl.BlockSpec((bt, tt, d), lambda ib, it: (ib, it, 0))
        xo_shape = jax.ShapeDtypeStruct((b, t, d), F32)
    else:
        xo_spec = pl.BlockSpec((tt, bt, d), lambda ib, it: (it, ib, 0))
        xo_shape = jax.ShapeDtypeStruct((t, b, d), F32)
    row = lambda a: a.reshape(1, -1)
    x_spec = pl.BlockSpec((tt, bt, d), lambda ib, it: (it, ib, 0))
    return pl.pallas_call(
        kern,
        grid=(nb, nt),
        in_specs=[
            x_spec,
            _const_spec((1, d)),
            _const_spec((d // 2, 3 * c)),
            _const_spec((1, 3 * c)),
            _const_spec((CCM_CONV_W, c)),
            _const_spec((1, c)),
            _const_spec((1, c)),
            _const_spec((1, c)),
            _const_spec((c // 2, d)),
            _const_spec((1, d)),
            _const_spec((1, d)),
            pl.BlockSpec((CCM_CONV_W - 1, bt, c), lambda ib, it: (0, ib, 0)),
        ],
        out_specs=[
            xo_spec,
            pl.BlockSpec((CCM_CONV_W - 1, bt, c), lambda ib, it: (0, ib, 0)),
        ],
        out_shape=[
            xo_shape,
            jax.ShapeDtypeStruct((CCM_CONV_W - 1, b, c), F32),
        ],
        scratch_shapes=[
            pltpu.VMEM((p + r + CONV_FIR_PAD * bt, c), F32),
            pltpu.VMEM((r, c), F32),
            pltpu.VMEM((r, c), F32),
            pltpu.VMEM((r, c), BF16),
            pltpu.VMEM((CCM_CONV_W, SUBLANES, c), F32),
        ],
        compiler_params=_compiler_params(),
        name="ccm_layer",
    )(x, row(g), w_in, row(b_in), dw_w, row(dw_b), row(ln_g), row(ln_b), w_out, row(b_out),
      row(final_g), conv0)


def kernel(x_prompt, x_sample, state_lru_conv, state_lru_h, state_ccm_conv, norm_g, final_norm_g,
           lru_w_in, lru_conv_w, lru_conv_b, lru_w_a, lru_b_a, lru_w_i, lru_b_i, lru_lam, lru_w_out,
           ccm_w_in, ccm_b_in, ccm_dw_w, ccm_dw_b, ccm_ln_g, ccm_ln_b, ccm_w_out, ccm_b_out):
    depth = norm_g.shape[0]
    bp = x_prompt.shape[0]
    c = lru_w_out.shape[1]
    assert depth % 2 == 0
    tm = lambda a: jnp.swapaxes(a, -3, -2)

    xp = x_prompt
    xs = tm(x_sample)
    s_lru_conv = tm(state_lru_conv)
    s_ccm_conv = tm(state_ccm_conv)
    z_lru_conv = jnp.zeros((LRU_CONV_W - 1, bp, c), F32)
    z_lru_h = jnp.zeros((bp, c), F32)
    z_ccm_conv = jnp.zeros((CCM_CONV_W - 1, bp, c), F32)

    lru_w_in_b = _pack_rows(lru_w_in)
    lru_w_out_b = _pack_rows(lru_w_out)
    lru_w_gate_b = _pack_rows(jnp.concatenate([lru_w_a, lru_w_i], axis=-1))
    ccm_w_in_b = _pack_rows(ccm_w_in)
    ccm_w_out_b = _pack_rows(ccm_w_out)

    lru_conv_p, lru_h_p, ccm_conv_p = [], [], []
    lru_conv_s, lru_h_s, ccm_conv_s = [], [], []
    for l in range(depth):
        j = l // 2
        if l % 2 == 0:
            lw = (norm_g[l], lru_w_in_b[j], lru_conv_w[j], lru_conv_b[j], lru_w_gate_b[j],
                  lru_b_a[j], lru_b_i[j], lru_lam[j], lru_w_out_b[j])
            xp, cb, hl = _lru_layer(xp, z_lru_conv, z_lru_h, *lw, tt=32, bt=bp, batch_major_in=l == 0)
            lru_conv_p.append(cb)
            lru_h_p.append(hl)
            xs, cb, hl = _lru_layer(xs, s_lru_conv[j], state_lru_h[j], *lw, tt=xs.shape[0], bt=xs.shape[1])
            lru_conv_s.append(cb)
            lru_h_s.append(hl)
        else:
            last = l == depth - 1
            cw = (norm_g[l], ccm_w_in_b[j], ccm_b_in[j], ccm_dw_w[j], ccm_dw_b[j], ccm_ln_g[j],
                  ccm_ln_b[j], ccm_w_out_b[j], ccm_b_out[j], final_norm_g)
            xp, cb = _ccm_layer(xp, z_ccm_conv, *cw, tt=64, bt=bp, final_norm=last, batch_major_out=last)
            ccm_conv_p.append(cb)
            xs, cb = _ccm_layer(xs, s_ccm_conv[j], *cw, tt=xs.shape[0], bt=16, final_norm=last)
            ccm_conv_s.append(cb)
    return (xp, tm(xs),
            tm(jnp.stack(lru_conv_p)), jnp.stack(lru_h_p), tm(jnp.stack(ccm_conv_p)),
            tm(jnp.stack(lru_conv_s)), jnp.stack(lru_h_s), tm(jnp.stack(ccm_conv_s)))
```

```python
import functools

import jax
import jax.numpy as jnp
from jax import lax
from jax.experimental import pallas as pl
from jax.experimental.pallas import tpu as pltpu

F32 = jnp.float32
BF16 = jnp.bfloat16

RMS_EPS = 1e-6
LN_EPS = 1e-5
LRU_C = 8.0
LRU_BLOCK = 128
LRU_CONV_W = 4
CCM_CONV_W = 31

V7X_VMEM_BYTES = 64 * 1024 * 1024
SUBLANES = 8
LANES = 128
VMEM_LIMIT_BYTES = V7X_VMEM_BYTES - 8 * 1024 * 1024

NORM_ROW_TILE = 32
CONV_FIR_LEVELS = 2
CONV_GROUP_STEPS = 64
CONV_FIR_PAD = 2 ** CONV_FIR_LEVELS - 1
MXU_COLS = 256
MIN_OUT_ROWS = 128
PACK_BLOCK_ROWS = 1024
PACK_BLOCK_COLS = 1024
LRU_PROMPT_STEPS = 32
CCM_PROMPT_STEPS = 64
CCM_DECODE_BATCH_TILE = 16


def _rmsnorm(x, g):
    ms = jnp.mean(x * x, axis=-1, keepdims=True)
    return (x * lax.rsqrt(ms + RMS_EPS)) * g


def _silu(x):
    return x * jax.nn.sigmoid(x)


def _bf16_bits(x):
    u = pltpu.bitcast(x, jnp.uint32)
    return (u + jnp.uint32(0x7FFF) + ((u >> 16) & jnp.uint32(1))) >> 16


def _pack_kernel(w_ref, o_ref, stage_s):
    kb = w_ref.shape[0]
    for j in range(w_ref.shape[1] // LANES):
        cols = slice(j * LANES, (j + 1) * LANES)
        stage_s[j] = w_ref[:, cols]
        lo = _bf16_bits(stage_s[j, pl.ds(0, kb // 2, stride=2), :])
        hi = _bf16_bits(stage_s[j, pl.ds(1, kb // 2, stride=2), :])
        o_ref[:, cols] = lo | (hi << 16)


def _pack_rows(w):
    *lead, k, n = w.shape
    assert k % 2 == 0
    w2 = w.reshape(-1, n)
    rows = w2.shape[0]
    kb, nb = min(rows, PACK_BLOCK_ROWS), min(n, PACK_BLOCK_COLS)
    assert rows % kb == 0 and n % nb == 0 and kb % (2 * SUBLANES) == 0 and nb % LANES == 0
    packed = pl.pallas_call(
        _pack_kernel,
        grid=(rows // kb, n // nb),
        in_specs=[pl.BlockSpec((kb, nb), lambda a, b: (a, b))],
        out_specs=pl.BlockSpec((kb // 2, nb), lambda a, b: (a, b)),
        out_shape=jax.ShapeDtypeStruct((rows // 2, n), jnp.uint32),
        scratch_shapes=[pltpu.VMEM((nb // LANES, kb, LANES), F32)],
        name="pack_weights",
    )(w2)
    return packed.reshape(*lead, k // 2, n)


def _rows_bf16(packed):
    return pltpu.bitcast(packed, BF16)


def _interleave(front, back):
    merged, i = [], 0
    for j, item in enumerate(back):
        while i < len(front) and i * len(back) <= j * len(front):
            merged.append(front[i])
            i += 1
        merged.append(item)
    merged.extend(front[i:])
    return merged


def _run_halves(front, back, pipelined):
    for part in (_interleave(front, back) if pipelined else front + back):
        part()


def _lru_kernel(x_ref, g_ref, win_ref, cw_ref, cb_ref, wg_ref, ba_ref, bi_ref,
                lam_ref, wout_ref, conv0_ref, h0_ref,
                xo_ref, convo_ref, ho_ref,
                xp_s, z_s, xr_s, hn_s, y_s, acc_s, h_s, *, tt, bt, nt, batch_major_in):
    d = x_ref.shape[-1]
    c = xp_s.shape[-1]
    r = tt * bt
    p = (LRU_CONV_W - 1) * bt
    pipelined = nt > 1
    nsteps = nt + 1 if pipelined else nt
    xb_off, xa_off = (p, p + r) if pipelined else (p, p)
    zb_off, za_off = (0, r) if pipelined else (0, 0)
    it = pl.program_id(1)

    @pl.when(it == 0)
    def _():
        if pipelined:
            xp_s[0:p + r, :] = jnp.zeros((p + r, c), F32)
            z_s[0:r, :] = jnp.zeros((r, c), F32)
            xr_s[0:r, :] = jnp.zeros((r, d), F32)
            h_s[...] = jnp.zeros(h_s.shape, F32)
        else:
            xp_s[0:p, :] = conv0_ref[...].reshape(p, c)
            h_s[...] = h0_ref[...]

    def front_norm():
        if batch_major_in:
            x = pltpu.einshape("btd->(tb)d", x_ref[...])
        else:
            x = x_ref[...].reshape(r, d)
        xr_s[za_off:za_off + r, :] = x
        hn_s[...] = _rmsnorm(x, g_ref[...]).astype(BF16)

    def front_x(j):
        cols = slice(j * MXU_COLS, (j + 1) * MXU_COLS)
        xp_s[xa_off:xa_off + r, cols] = jnp.dot(hn_s[...], _rows_bf16(win_ref[:, cols]), preferred_element_type=F32)

    def front_z(j):
        cols = slice(j * MXU_COLS, (j + 1) * MXU_COLS)
        wcols = slice(c + j * MXU_COLS, c + (j + 1) * MXU_COLS)
        z_s[za_off:za_off + r, cols] = _silu(jnp.dot(hn_s[...], _rows_bf16(win_ref[:, wcols]), preferred_element_type=F32))

    def back_head(h):
        cols = slice(h * LRU_BLOCK, (h + 1) * LRU_BLOCK)
        base = xb_off - p
        xc = cb_ref[:, cols] + cw_ref[0:1, cols] * xp_s[base:base + r, cols]
        for k in range(1, LRU_CONV_W):
            xc = xc + cw_ref[k:k + 1, cols] * xp_s[base + k * bt:base + k * bt + r, cols]
        gates = jnp.dot(xc.astype(BF16), _rows_bf16(wg_ref[h]), preferred_element_type=F32)
        rg = jax.nn.sigmoid(gates[:, 0:LRU_BLOCK] + ba_ref[:, cols])
        ig = jax.nn.sigmoid(gates[:, LRU_BLOCK:] + bi_ref[:, cols])
        log_a = (-LRU_C * rg) * jax.nn.softplus(-lam_ref[:, cols])
        a = jnp.exp(log_a)
        b = jnp.sqrt(-jnp.tanh(log_a) * (a * a + 1.0)) * (ig * xc)
        hcur = h_s[:, cols]
        hs = []
        for t in range(tt):
            hcur = a[t * bt:(t + 1) * bt] * hcur + b[t * bt:(t + 1) * bt]
            hs.append(hcur)
        h_s[:, cols] = hcur
        hall = jnp.concatenate(hs, axis=0)
        y_s[:, cols] = (hall * z_s[zb_off:zb_off + r, cols]).astype(BF16)

    def back_out(j):
        rows = slice(j * MXU_COLS, (j + 1) * MXU_COLS)
        wrows = slice(j * MXU_COLS // 2, (j + 1) * MXU_COLS // 2)
        part = jnp.dot(y_s[:, rows], _rows_bf16(wout_ref[wrows, :]), preferred_element_type=F32)
        if j == 0:
            acc_s[...] = xr_s[zb_off:zb_off + r, :] + part
        elif j < c // MXU_COLS - 1:
            acc_s[...] = acc_s[...] + part
        else:
            xo_ref[...] = (acc_s[...] + part).reshape(tt, bt, d)

    heads_per_chunk = MXU_COLS // LRU_BLOCK
    front = [front_norm]
    for j in range(c // MXU_COLS):
        front += [functools.partial(front_x, j), functools.partial(front_z, j)]
    back = []
    for j in range(c // MXU_COLS):
        back += [functools.partial(back_head, j * heads_per_chunk + i) for i in range(heads_per_chunk)]
        back.append(functools.partial(back_out, j))
    _run_halves(front, back, pipelined)

    if pipelined:
        xp_s[0:p + r, :] = xp_s[r:p + 2 * r, :]
        z_s[0:r, :] = z_s[r:2 * r, :]
        xr_s[0:r, :] = xr_s[r:2 * r, :]

        @pl.when(it == 0)
        def _():
            xp_s[0:p, :] = conv0_ref[...].reshape(p, c)
            h_s[...] = h0_ref[...]

    @pl.when(it == nsteps - 1)
    def _():
        tail = 0 if pipelined else r
        convo_ref[...] = xp_s[tail:tail + p, :].reshape(LRU_CONV_W - 1, bt, c)
        ho_ref[...] = h_s[...]


def _ccm_kernel(x_ref, g_ref, win_ref, bin_ref, dw_ref, dwb_ref, lng_ref, lnb_ref,
                wout_ref, bout_ref, fg_ref, conv0_ref,
                xo_ref, convo_ref,
                cp_s, z_s, hn_s, y_s, s_s, wb_s, *, tt, bt, nt, final_norm, batch_major_out):
    d = x_ref.shape[-1]
    c = z_s.shape[-1]
    r = tt * bt
    p = (CCM_CONV_W - 1) * bt
    it = pl.program_id(1)

    @pl.when(it == 0)
    def _():
        for k in range(CCM_CONV_W):
            wb_s[k] = jnp.broadcast_to(dw_ref[k:k + 1, :], (SUBLANES, c))
        cp_s[0:p, :] = conv0_ref[...].reshape(p, c)
        cp_s[p + r:p + r + CONV_FIR_PAD * bt, :] = jnp.zeros((CONV_FIR_PAD * bt, c), F32)

    def lane_chunk(start, width):
        return pl.ds(pl.multiple_of(start, width), width)

    nsub = bt // SUBLANES
    go = min(tt, CONV_GROUP_STEPS)

    def tap_sum(ta, tb):
        return tb if ta is None else (ta if tb is None else ta + tb)

    def direct(taps, x, v0, n_out):
        accs = [None] * n_out
        for i in range(n_out + len(taps) - 1):
            xi = x(v0 + i)
            for v in range(max(0, i - len(taps) + 1), min(n_out, i + 1)):
                if taps[i - v] is not None:
                    term = xi * taps[i - v][None]
                    accs[v] = term if accs[v] is None else accs[v] + term
        return accs

    def split(taps):
        taps = list(taps) + [None] * (len(taps) % 2)
        te, to = taps[0::2], taps[1::2]
        return te, to, [tap_sum(ta, tb) for ta, tb in zip(te, to)]

    def corr(taps, x, v0, n_out, levels):
        if levels == 0 or n_out < 2:
            return direct(taps, x, v0, n_out)
        te, to, ts = split(taps)
        n2 = (n_out + 1) // 2
        a = corr(te, lambda i: x(2 * i), v0 // 2, n2 + 1, levels - 1)
        b = corr(to, lambda i: x(2 * i + 1), v0 // 2, n2, levels - 1)
        s = corr(ts, lambda i: x(2 * i + 1) + x(2 * i + 2), v0 // 2, n2, levels - 1)
        out = []
        for v in range(n2):
            out += [a[v] + b[v], s[v] - a[v + 1] - b[v]]
        return out[:n_out]

    def conv_chunk(jc, carry):
        cols = lane_chunk(jc * LANES, LANES)

        def step(u):
            return cp_s[u * bt:(u + 1) * bt, cols].reshape(nsub, SUBLANES, LANES)

        def put(t, val):
            y_s[t * bt:(t + 1) * bt, cols] = val.reshape(bt, LANES)

        bias = jnp.broadcast_to(dwb_ref[:, cols], (bt, LANES)).reshape(nsub, SUBLANES, LANES)
        te, to, ts = split([wb_s[k, :, cols] for k in range(CCM_CONV_W)])
        for v0 in range(0, tt, go):
            n2 = go // 2
            a = corr(te, lambda i: step(2 * i), v0 // 2, n2 + 1, CONV_FIR_LEVELS - 1)
            b = corr(to, lambda i: step(2 * i + 1), v0 // 2, n2, CONV_FIR_LEVELS - 1)
            odd0 = []
            for v in range(n2):
                put(v0 + 2 * v, bias + (a[v] + b[v]))
                odd0.append(bias - (a[v + 1] + b[v]))
            s = corr(ts, lambda i: step(2 * i + 1) + step(2 * i + 2), v0 // 2, n2, CONV_FIR_LEVELS - 1)
            for v in range(n2):
                put(v0 + 2 * v + 1, odd0[v] + s[v])
        return carry

    rt = NORM_ROW_TILE

    def norm_tile(i):
        r0 = i * rt
        y = y_s[r0:r0 + rt, :]
        mu = jnp.mean(y, axis=-1, keepdims=True)
        yc = y - mu
        var = jnp.mean(yc * yc, axis=-1, keepdims=True)
        yn = yc * lax.rsqrt(var + LN_EPS) * lng_ref[...] + lnb_ref[...]
        s_s[r0:r0 + rt, :] = (_silu(yn) * z_s[r0:r0 + rt, :]).astype(BF16)

    def out_rows(r0, nrows):
        out = jnp.dot(s_s[r0:r0 + nrows, :], _rows_bf16(wout_ref[...]), preferred_element_type=F32) + bout_ref[...]
        xn = x_ref[r0 // bt:(r0 + nrows) // bt].reshape(nrows, d) + out
        if final_norm:
            xn = _rmsnorm(xn, fg_ref[...])
        if batch_major_out:
            xo_ref[:, r0 // bt:(r0 + nrows) // bt, :] = pltpu.einshape("(tb)d->btd", xn, b=bt)
        else:
            xo_ref[r0 // bt:(r0 + nrows) // bt] = xn.reshape(nrows // bt, bt, d)

    hn_s[...] = _rmsnorm(x_ref[...].reshape(r, d), g_ref[...]).astype(BF16)

    def in_proj(part, rows=slice(None)):
        wcols = slice(part * c, (part + 1) * c)
        return jnp.dot(hn_s[rows, :], _rows_bf16(win_ref[:, wcols]), preferred_element_type=F32) + bin_ref[:, wcols]

    cp_s[p:p + r, :] = in_proj(0) * jax.nn.sigmoid(in_proj(1))
    lax.fori_loop(0, c // LANES, conv_chunk, 0)
    group = min(r, max(r // 4, MIN_OUT_ROWS))
    for g0 in range(0, r, group):
        z_s[g0:g0 + group, :] = _silu(in_proj(2, slice(g0, g0 + group)))
        for i in range(g0 // rt, (g0 + group) // rt):
            norm_tile(i)
        out_rows(g0, group)

    @pl.when(it == nt - 1)
    def _():
        convo_ref[...] = cp_s[r:r + p, :].reshape(CCM_CONV_W - 1, bt, c)

    if nt > 1:
        cp_s[0:p, :] = cp_s[r:r + p, :]


def _const_spec(shape):
    nd = len(shape)
    return pl.BlockSpec(shape, lambda ib, it, nd=nd: (0,) * nd, pipeline_mode=pl.Buffered(1))


def _stream_specs(tt, bt, d, nt, batch_major_in):
    if nt > 1:
        in_map = lambda ib, it: (jnp.minimum(it, nt - 1), ib, 0)
        out_map = lambda ib, it: (jnp.maximum(it - 1, 0), ib, 0)
        nsteps = nt + 1
    else:
        in_map = out_map = lambda ib, it: (it, ib, 0)
        nsteps = nt
    if batch_major_in:
        x_in = pl.BlockSpec((bt, tt, d), lambda ib, it: tuple(in_map(ib, it)[i] for i in (1, 0, 2)))
    else:
        x_in = pl.BlockSpec((tt, bt, d), in_map)
    return x_in, pl.BlockSpec((tt, bt, d), out_map), nsteps


def _compiler_params():
    return pltpu.CompilerParams(
        dimension_semantics=("arbitrary", "arbitrary"),
        vmem_limit_bytes=VMEM_LIMIT_BYTES)


def _lru_layer(x, conv0, h0, g, w_in, conv_w, conv_b, w_gate, b_a, b_i, lam, w_out, *, tt, bt,
               batch_major_in=False):
    (b, t, d) = x.shape if batch_major_in else (x.shape[1], x.shape[0], x.shape[2])
    c = conv_w.shape[-1]
    assert w_in.shape == (d // 2, 2 * c) and w_out.shape == (c // 2, d)
    nt = t // tt
    nb = b // bt
    r = tt * bt
    p = (LRU_CONV_W - 1) * bt
    assert t % tt == 0 and b % bt == 0 and bt % SUBLANES == 0 and r % (2 * SUBLANES) == 0
    assert nt == 1 or r >= p
    assert c % MXU_COLS == 0 and c // MXU_COLS >= 2 and MXU_COLS % LRU_BLOCK == 0
    nblk = 2 if nt > 1 else 1
    x_in, x_out, nsteps = _stream_specs(tt, bt, d, nt, batch_major_in)
    kern = functools.partial(_lru_kernel, tt=tt, bt=bt, nt=nt, batch_major_in=batch_major_in)
    row = lambda a: a.reshape(1, -1)
    return pl.pallas_call(
        kern,
        grid=(nb, nsteps),
        in_specs=[
            x_in,
            _const_spec((1, d)),
            _const_spec((d // 2, 2 * c)),
            _const_spec((LRU_CONV_W, c)),
            _const_spec((1, c)),
            _const_spec(w_gate.shape),
            _const_spec((1, c)),
            _const_spec((1, c)),
            _const_spec((1, c)),
            _const_spec((c // 2, d)),
            pl.BlockSpec((LRU_CONV_W - 1, bt, c), lambda ib, it: (0, ib, 0)),
            pl.BlockSpec((bt, c), lambda ib, it: (ib, 0)),
        ],
        out_specs=[
            x_out,
            pl.BlockSpec((LRU_CONV_W - 1, bt, c), lambda ib, it: (0, ib, 0)),
            pl.BlockSpec((bt, c), lambda ib, it: (ib, 0)),
        ],
        out_shape=[
            jax.ShapeDtypeStruct((t, b, d), F32),
            jax.ShapeDtypeStruct((LRU_CONV_W - 1, b, c), F32),
            jax.ShapeDtypeStruct((b, c), F32),
        ],
        scratch_shapes=[
            pltpu.VMEM((p + nblk * r, c), F32),
            pltpu.VMEM((nblk * r, c), F32),
            pltpu.VMEM((nblk * r, d), F32),
            pltpu.VMEM((r, d), BF16),
            pltpu.VMEM((r, c), BF16),
            pltpu.VMEM((r, d), F32),
            pltpu.VMEM((bt, c), F32),
        ],
        compiler_params=_compiler_params(),
        name="lru_layer",
    )(x, row(g), w_in, conv_w, row(conv_b), w_gate, row(b_a), row(b_i), row(lam), w_out, conv0, h0)


def _ccm_layer(x, conv0, g, w_in, b_in, dw_w, dw_b, ln_g, ln_b, w_out, b_out, final_g, *,
               tt, bt, final_norm, batch_major_out=False):
    t, b, d = x.shape
    c = dw_w.shape[-1]
    assert w_in.shape == (d // 2, 3 * c) and w_out.shape == (c // 2, d)
    nt = t // tt
    nb = b // bt
    r = tt * bt
    p = (CCM_CONV_W - 1) * bt
    go = min(tt, CONV_GROUP_STEPS)
    assert t % tt == 0 and b % bt == 0 and bt % SUBLANES == 0
    assert tt % go == 0 and go % 2 ** CONV_FIR_LEVELS == 0 and r % NORM_ROW_TILE == 0 and c % MXU_COLS == 0
    assert nt == 1 or r >= p
    assert r % min(r, max(r // 4, MIN_OUT_ROWS)) == 0
    kern = functools.partial(_ccm_kernel, tt=tt, bt=bt, nt=nt, final_norm=final_norm,
                             batch_major_out=batch_major_out)
    if batch_major_out:
        xo_spec = pl.BlockSpec((bt, tt, d), lambda ib, it: (ib, it, 0))
        xo_shape = jax.ShapeDtypeStruct((b, t, d), F32)
    else:
        xo_spec = pl.BlockSpec((tt, bt, d), lambda ib, it: (it, ib, 0))
        xo_shape = jax.ShapeDtypeStruct((t, b, d), F32)
    row = lambda a: a.reshape(1, -1)
    x_spec = pl.BlockSpec((tt, bt, d), lambda ib, it: (it, ib, 0))
    return pl.pallas_call(
        kern,
        grid=(nb, nt),
        in_specs=[
            x_spec,
            _const_spec((1, d)),
            _const_spec((d // 2, 3 * c)),
            _const_spec((1, 3 * c)),
            _const_spec((CCM_CONV_W, c)),
            _const_spec((1, c)),
            _const_spec((1, c)),
            _const_spec((1, c)),
            _const_spec((c // 2, d)),
            _const_spec((1, d)),
            _const_spec((1, d)),
            pl.BlockSpec((CCM_CONV_W - 1, bt, c), lambda ib, it: (0, ib, 0)),
        ],
        out_specs=[
            xo_spec,
            pl.BlockSpec((CCM_CONV_W - 1, bt, c), lambda ib, it: (0, ib, 0)),
        ],
        out_shape=[
            xo_shape,
            jax.ShapeDtypeStruct((CCM_CONV_W - 1, b, c), F32),
        ],
        scratch_shapes=[
            pltpu.VMEM((p + r + CONV_FIR_PAD * bt, c), F32),
            pltpu.VMEM((r, c), F32),
            pltpu.VMEM((r, d), BF16),
            pltpu.VMEM((r, c), F32),
            pltpu.VMEM((r, c), BF16),
            pltpu.VMEM((CCM_CONV_W, SUBLANES, c), F32),
        ],
        compiler_params=_compiler_params(),
        name="ccm_layer",
    )(x, row(g), w_in, row(b_in), dw_w, row(dw_b), row(ln_g), row(ln_b), w_out, row(b_out),
      row(final_g), conv0)


def kernel(x_prompt, x_sample, state_lru_conv, state_lru_h, state_ccm_conv, norm_g, final_norm_g,
           lru_w_in, lru_conv_w, lru_conv_b, lru_w_a, lru_b_a, lru_w_i, lru_b_i, lru_lam, lru_w_out,
           ccm_w_in, ccm_b_in, ccm_dw_w, ccm_dw_b, ccm_ln_g, ccm_ln_b, ccm_w_out, ccm_b_out):
    depth = norm_g.shape[0]
    bp = x_prompt.shape[0]
    c = lru_w_out.shape[1]
    assert depth % 2 == 0
    tm = lambda a: jnp.swapaxes(a, -3, -2)

    xp = x_prompt
    xs = tm(x_sample)
    s_lru_conv = tm(state_lru_conv)
    s_ccm_conv = tm(state_ccm_conv)
    z_lru_conv = jnp.zeros((LRU_CONV_W - 1, bp, c), F32)
    z_lru_h = jnp.zeros((bp, c), F32)
    z_ccm_conv = jnp.zeros((CCM_CONV_W - 1, bp, c), F32)

    lru_w_in_b = _pack_rows(lru_w_in)
    lru_w_out_b = _pack_rows(lru_w_out)
    lru_w_gate_b = _pack_rows(jnp.concatenate([lru_w_a, lru_w_i], axis=-1))
    ccm_w_in_b = _pack_rows(ccm_w_in)
    ccm_w_out_b = _pack_rows(ccm_w_out)

    lru_conv_p, lru_h_p, ccm_conv_p = [], [], []
    lru_conv_s, lru_h_s, ccm_conv_s = [], [], []
    for l in range(depth):
        j = l // 2
        if l % 2 == 0:
            lw = (norm_g[l], lru_w_in_b[j], lru_conv_w[j], lru_conv_b[j], lru_w_gate_b[j],
                  lru_b_a[j], lru_b_i[j], lru_lam[j], lru_w_out_b[j])
            xp, cb, hl = _lru_layer(xp, z_lru_conv, z_lru_h, *lw, tt=LRU_PROMPT_STEPS, bt=bp,
                                    batch_major_in=l == 0)
            lru_conv_p.append(cb)
            lru_h_p.append(hl)
            xs, cb, hl = _lru_layer(xs, s_lru_conv[j], state_lru_h[j], *lw, tt=xs.shape[0], bt=xs.shape[1])
            lru_conv_s.append(cb)
            lru_h_s.append(hl)
        else:
            last = l == depth - 1
            cw = (norm_g[l], ccm_w_in_b[j], ccm_b_in[j], ccm_dw_w[j], ccm_dw_b[j], ccm_ln_g[j],
                  ccm_ln_b[j], ccm_w_out_b[j], ccm_b_out[j], final_norm_g)
            xp, cb = _ccm_layer(xp, z_ccm_conv, *cw, tt=CCM_PROMPT_STEPS, bt=bp, final_norm=last,
                                batch_major_out=last)
            ccm_conv_p.append(cb)
            xs, cb = _ccm_layer(xs, s_ccm_conv[j], *cw, tt=xs.shape[0], bt=CCM_DECODE_BATCH_TILE,
                                final_norm=last)
            ccm_conv_s.append(cb)
    return (xp, tm(xs),
            tm(jnp.stack(lru_conv_p)), jnp.stack(lru_h_p), tm(jnp.stack(ccm_conv_p)),
            tm(jnp.stack(lru_conv_s)), jnp.stack(lru_h_s), tm(jnp.stack(ccm_conv_s)))
```

```python
import functools

import jax
import jax.numpy as jnp
from jax import lax
from jax.experimental import pallas as pl
from jax.experimental.pallas import tpu as pltpu

F32 = jnp.float32
BF16 = jnp.bfloat16

RMS_EPS = 1e-6
LN_EPS = 1e-5
LRU_C = 8.0
LRU_BLOCK = 128
LRU_CONV_W = 4
CCM_CONV_W = 31

V7X_VMEM_BYTES = 64 * 1024 * 1024
SUBLANES = 8
LANES = 128
VMEM_LIMIT_BYTES = V7X_VMEM_BYTES - 8 * 1024 * 1024

NORM_ROW_TILE = 32
CONV_FIR_LEVELS = 2
CONV_GROUP_STEPS = 64
CONV_FIR_PAD = 2 ** CONV_FIR_LEVELS - 1
MXU_COLS = 256
MIN_OUT_ROWS = 128
PACK_BLOCK_ROWS = 1024
PACK_BLOCK_COLS = 1024
LRU_PROMPT_STEPS = 32
CCM_PROMPT_STEPS = 64
CCM_DECODE_BATCH_TILE = 16


def _rmsnorm(x, g):
    ms = jnp.mean(x * x, axis=-1, keepdims=True)
    return (x * lax.rsqrt(ms + RMS_EPS)) * g


def _silu(x):
    return x * jax.nn.sigmoid(x)


def _bf16_bits(x):
    u = pltpu.bitcast(x, jnp.uint32)
    return (u + jnp.uint32(0x7FFF) + ((u >> 16) & jnp.uint32(1))) >> 16


def _pack_kernel(w_ref, o_ref, stage_s):
    kb = w_ref.shape[0]
    for j in range(w_ref.shape[1] // LANES):
        cols = slice(j * LANES, (j + 1) * LANES)
        stage_s[j] = w_ref[:, cols]
        lo = _bf16_bits(stage_s[j, pl.ds(0, kb // 2, stride=2), :])
        hi = _bf16_bits(stage_s[j, pl.ds(1, kb // 2, stride=2), :])
        o_ref[:, cols] = lo | (hi << 16)


def _pack_rows(w):
    *lead, k, n = w.shape
    assert k % 2 == 0
    w2 = w.reshape(-1, n)
    rows = w2.shape[0]
    kb, nb = min(rows, PACK_BLOCK_ROWS), min(n, PACK_BLOCK_COLS)
    assert rows % kb == 0 and n % nb == 0 and kb % (2 * SUBLANES) == 0 and nb % LANES == 0
    packed = pl.pallas_call(
        _pack_kernel,
        grid=(rows // kb, n // nb),
        in_specs=[pl.BlockSpec((kb, nb), lambda a, b: (a, b))],
        out_specs=pl.BlockSpec((kb // 2, nb), lambda a, b: (a, b)),
        out_shape=jax.ShapeDtypeStruct((rows // 2, n), jnp.uint32),
        scratch_shapes=[pltpu.VMEM((nb // LANES, kb, LANES), F32)],
        name="pack_weights",
    )(w2)
    return packed.reshape(*lead, k // 2, n)


def _rows_bf16(packed):
    return pltpu.bitcast(packed, BF16)


def _interleave(front, back):
    merged, i = [], 0
    for j, item in enumerate(back):
        while i < len(front) and i * len(back) <= j * len(front):
            merged.append(front[i])
            i += 1
        merged.append(item)
    merged.extend(front[i:])
    return merged


def _run_halves(front, back, pipelined):
    for part in (_interleave(front, back) if pipelined else front + back):
        part()


def _lru_kernel(x_ref, g_ref, win_ref, cw_ref, cb_ref, wg_ref, ba_ref, bi_ref,
                lam_ref, wout_ref, conv0_ref, h0_ref,
                xo_ref, convo_ref, ho_ref,
                xp_s, z_s, xr_s, hn_s, y_s, acc_s, h_s, *, tt, bt, nt, batch_major_in):
    d = x_ref.shape[-1]
    c = xp_s.shape[-1]
    r = tt * bt
    p = (LRU_CONV_W - 1) * bt
    pipelined = nt > 1
    nsteps = nt + 1 if pipelined else nt
    xb_off, xa_off = (p, p + r) if pipelined else (p, p)
    zb_off, za_off = (0, r) if pipelined else (0, 0)
    it = pl.program_id(1)

    @pl.when(it == 0)
    def _():
        if pipelined:
            xp_s[0:p + r, :] = jnp.zeros((p + r, c), F32)
            z_s[0:r, :] = jnp.zeros((r, c), F32)
            xr_s[0:r, :] = jnp.zeros((r, d), F32)
            h_s[...] = jnp.zeros(h_s.shape, F32)
        else:
            xp_s[0:p, :] = conv0_ref[...].reshape(p, c)
            h_s[...] = h0_ref[...]

    def front_norm():
        if batch_major_in:
            x = pltpu.einshape("btd->(tb)d", x_ref[...])
        else:
            x = x_ref[...].reshape(r, d)
        xr_s[za_off:za_off + r, :] = x
        hn_s[...] = _rmsnorm(x, g_ref[...]).astype(BF16)

    def front_x(j):
        cols = slice(j * MXU_COLS, (j + 1) * MXU_COLS)
        xp_s[xa_off:xa_off + r, cols] = jnp.dot(hn_s[...], _rows_bf16(win_ref[:, cols]), preferred_element_type=F32)

    def front_z(j):
        cols = slice(j * MXU_COLS, (j + 1) * MXU_COLS)
        wcols = slice(c + j * MXU_COLS, c + (j + 1) * MXU_COLS)
        z_s[za_off:za_off + r, cols] = _silu(jnp.dot(hn_s[...], _rows_bf16(win_ref[:, wcols]), preferred_element_type=F32))

    def back_head(h):
        cols = slice(h * LRU_BLOCK, (h + 1) * LRU_BLOCK)
        base = xb_off - p
        xc = cb_ref[:, cols] + cw_ref[0:1, cols] * xp_s[base:base + r, cols]
        for k in range(1, LRU_CONV_W):
            xc = xc + cw_ref[k:k + 1, cols] * xp_s[base + k * bt:base + k * bt + r, cols]
        gates = jnp.dot(xc.astype(BF16), _rows_bf16(wg_ref[h]), preferred_element_type=F32)
        rg = jax.nn.sigmoid(gates[:, 0:LRU_BLOCK] + ba_ref[:, cols])
        ig = jax.nn.sigmoid(gates[:, LRU_BLOCK:] + bi_ref[:, cols])
        log_a = (-LRU_C * rg) * jax.nn.softplus(-lam_ref[:, cols])
        a = jnp.exp(log_a)
        b = jnp.sqrt(-jnp.tanh(log_a) * (a * a + 1.0)) * (ig * xc)
        hcur = h_s[:, cols]
        hs = []
        for t in range(tt):
            hcur = a[t * bt:(t + 1) * bt] * hcur + b[t * bt:(t + 1) * bt]
            hs.append(hcur)
        h_s[:, cols] = hcur
        hall = jnp.concatenate(hs, axis=0)
        y_s[:, cols] = (hall * z_s[zb_off:zb_off + r, cols]).astype(BF16)

    def back_out(j):
        rows = slice(j * MXU_COLS, (j + 1) * MXU_COLS)
        wrows = slice(j * MXU_COLS // 2, (j + 1) * MXU_COLS // 2)
        part = jnp.dot(y_s[:, rows], _rows_bf16(wout_ref[wrows, :]), preferred_element_type=F32)
        if j == 0:
            acc_s[...] = xr_s[zb_off:zb_off + r, :] + part
        elif j < c // MXU_COLS - 1:
            acc_s[...] = acc_s[...] + part
        else:
            xo_ref[...] = (acc_s[...] + part).reshape(tt, bt, d)

    heads_per_chunk = MXU_COLS // LRU_BLOCK
    front = [front_norm]
    for j in range(c // MXU_COLS):
        front += [functools.partial(front_x, j), functools.partial(front_z, j)]
    back = []
    for j in range(c // MXU_COLS):
        back += [functools.partial(back_head, j * heads_per_chunk + i) for i in range(heads_per_chunk)]
        back.append(functools.partial(back_out, j))
    _run_halves(front, back, pipelined)

    if pipelined:
        xp_s[0:p + r, :] = xp_s[r:p + 2 * r, :]
        z_s[0:r, :] = z_s[r:2 * r, :]
        xr_s[0:r, :] = xr_s[r:2 * r, :]

        @pl.when(it == 0)
        def _():
            xp_s[0:p, :] = conv0_ref[...].reshape(p, c)
            h_s[...] = h0_ref[...]

    @pl.when(it == nsteps - 1)
    def _():
        tail = 0 if pipelined else r
        convo_ref[...] = xp_s[tail:tail + p, :].reshape(LRU_CONV_W - 1, bt, c)
        ho_ref[...] = h_s[...]


def _ccm_kernel(x_ref, g_ref, win_ref, bin_ref, dw_ref, dwb_ref, lng_ref, lnb_ref,
                wout_ref, bout_ref, fg_ref, conv0_ref,
                xo_ref, convo_ref,
                cp_s, z_s, hn_s, y_s, s_s, wb_s, *, tt, bt, nt, final_norm, batch_major_out):
    d = x_ref.shape[-1]
    c = z_s.shape[-1]
    r = tt * bt
    p = (CCM_CONV_W - 1) * bt
    it = pl.program_id(1)

    @pl.when(it == 0)
    def _():
        for k in range(CCM_CONV_W):
            wb_s[k] = jnp.broadcast_to(dw_ref[k:k + 1, :], (SUBLANES, c))
        cp_s[0:p, :] = conv0_ref[...].reshape(p, c)
        cp_s[p + r:p + r + CONV_FIR_PAD * bt, :] = jnp.zeros((CONV_FIR_PAD * bt, c), F32)

    def lane_chunk(start, width):
        return pl.ds(pl.multiple_of(start, width), width)

    nsub = bt // SUBLANES
    go = min(tt, CONV_GROUP_STEPS)

    def tap_sum(ta, tb):
        return tb if ta is None else (ta if tb is None else ta + tb)

    def direct(taps, x, v0, n_out):
        accs = [None] * n_out
        for i in range(n_out + len(taps) - 1):
            xi = x(v0 + i)
            for v in range(max(0, i - len(taps) + 1), min(n_out, i + 1)):
                if taps[i - v] is not None:
                    term = xi * taps[i - v][None]
                    accs[v] = term if accs[v] is None else accs[v] + term
        return accs

    def split(taps):
        taps = list(taps) + [None] * (len(taps) % 2)
        te, to = taps[0::2], taps[1::2]
        return te, to, [tap_sum(ta, tb) for ta, tb in zip(te, to)]

    def corr(taps, x, v0, n_out, levels):
        if levels == 0 or n_out < 2:
            return direct(taps, x, v0, n_out)
        te, to, ts = split(taps)
        n2 = (n_out + 1) // 2
        a = corr(te, lambda i: x(2 * i), v0 // 2, n2 + 1, levels - 1)
        b = corr(to, lambda i: x(2 * i + 1), v0 // 2, n2, levels - 1)
        s = corr(ts, lambda i: x(2 * i + 1) + x(2 * i + 2), v0 // 2, n2, levels - 1)
        out = []
        for v in range(n2):
            out += [a[v] + b[v], s[v] - a[v + 1] - b[v]]
        return out[:n_out]

    def conv_chunk(jc, carry):
        cols = lane_chunk(jc * LANES, LANES)

        def step(u):
            return cp_s[u * bt:(u + 1) * bt, cols].reshape(nsub, SUBLANES, LANES)

        def put(t, val):
            y_s[t * bt:(t + 1) * bt, cols] = val.reshape(bt, LANES)

        bias = jnp.broadcast_to(dwb_ref[:, cols], (bt, LANES)).reshape(nsub, SUBLANES, LANES)
        te, to, ts = split([wb_s[k, :, cols] for k in range(CCM_CONV_W)])
        for v0 in range(0, tt, go):
            n2 = go // 2
            a = corr(te, lambda i: step(2 * i), v0 // 2, n2 + 1, CONV_FIR_LEVELS - 1)
            b = corr(to, lambda i: step(2 * i + 1), v0 // 2, n2, CONV_FIR_LEVELS - 1)
            odd0 = []
            for v in range(n2):
                put(v0 + 2 * v, bias + (a[v] + b[v]))
                odd0.append(bias - (a[v + 1] + b[v]))
            s = corr(ts, lambda i: step(2 * i + 1) + step(2 * i + 2), v0 // 2, n2, CONV_FIR_LEVELS - 1)
            for v in range(n2):
                put(v0 + 2 * v + 1, odd0[v] + s[v])
        return carry

    rt = NORM_ROW_TILE

    def norm_tile(i):
        r0 = i * rt
        y = y_s[r0:r0 + rt, :]
        mu = jnp.mean(y, axis=-1, keepdims=True)
        yc = y - mu
        var = jnp.mean(yc * yc, axis=-1, keepdims=True)
        yn = yc * lax.rsqrt(var + LN_EPS) * lng_ref[...] + lnb_ref[...]
        s_s[r0:r0 + rt, :] = (_silu(yn) * z_s[r0:r0 + rt, :]).astype(BF16)

    def out_rows(r0, nrows):
        out = jnp.dot(s_s[r0:r0 + nrows, :], _rows_bf16(wout_ref[...]), preferred_element_type=F32) + bout_ref[...]
        xn = x_ref[r0 // bt:(r0 + nrows) // bt].reshape(nrows, d) + out
        if final_norm:
            xn = _rmsnorm(xn, fg_ref[...])
        if batch_major_out:
            xo_ref[:, r0 // bt:(r0 + nrows) // bt, :] = pltpu.einshape("(tb)d->btd", xn, b=bt)
        else:
            xo_ref[r0 // bt:(r0 + nrows) // bt] = xn.reshape(nrows // bt, bt, d)

    hn_s[...] = _rmsnorm(x_ref[...].reshape(r, d), g_ref[...]).astype(BF16)

    def in_proj(part, g0):
        wcols = slice(part * c, (part + 1) * c)
        lhs = hn_s[g0:g0 + group, :]
        return jnp.dot(lhs, _rows_bf16(win_ref[:, wcols]), preferred_element_type=F32) + bin_ref[:, wcols]

    group = min(r, max(r // 4, MIN_OUT_ROWS))
    for g0 in range(0, r, group):
        cp_s[p + g0:p + g0 + group, :] = in_proj(0, g0) * jax.nn.sigmoid(in_proj(1, g0))
    lax.fori_loop(0, c // LANES, conv_chunk, 0)
    for g0 in range(0, r, group):
        z_s[g0:g0 + group, :] = _silu(in_proj(2, g0))
        for i in range(g0 // rt, (g0 + group) // rt):
            norm_tile(i)
        out_rows(g0, group)

    @pl.when(it == nt - 1)
    def _():
        convo_ref[...] = cp_s[r:r + p, :].reshape(CCM_CONV_W - 1, bt, c)

    if nt > 1:
        cp_s[0:p, :] = cp_s[r:r + p, :]


def _const_spec(shape):
    nd = len(shape)
    return pl.BlockSpec(shape, lambda ib, it, nd=nd: (0,) * nd, pipeline_mode=pl.Buffered(1))


def _stream_specs(tt, bt, d, nt, batch_major_in):
    if nt > 1:
        in_map = lambda ib, it: (jnp.minimum(it, nt - 1), ib, 0)
        out_map = lambda ib, it: (jnp.maximum(it - 1, 0), ib, 0)
        nsteps = nt + 1
    else:
        in_map = out_map = lambda ib, it: (it, ib, 0)
        nsteps = nt
    if batch_major_in:
        x_in = pl.BlockSpec((bt, tt, d), lambda ib, it: tuple(in_map(ib, it)[i] for i in (1, 0, 2)))
    else:
        x_in = pl.BlockSpec((tt, bt, d), in_map)
    return x_in, pl.BlockSpec((tt, bt, d), out_map), nsteps


def _compiler_params():
    return pltpu.CompilerParams(
        dimension_semantics=("arbitrary", "arbitrary"),
        vmem_limit_bytes=VMEM_LIMIT_BYTES)


def _lru_layer(x, conv0, h0, g, w_in, conv_w, conv_b, w_gate, b_a, b_i, lam, w_out, *, tt, bt,
               batch_major_in=False):
    (b, t, d) = x.shape if batch_major_in else (x.shape[1], x.shape[0], x.shape[2])
    c = conv_w.shape[-1]
    assert w_in.shape == (d // 2, 2 * c) and w_out.shape == (c // 2, d)
    nt = t // tt
    nb = b // bt
    r = tt * bt
    p = (LRU_CONV_W - 1) * bt
    assert t % tt == 0 and b % bt == 0 and bt % SUBLANES == 0 and r % (2 * SUBLANES) == 0
    assert nt == 1 or r >= p
    assert c % MXU_COLS == 0 and c // MXU_COLS >= 2 and MXU_COLS % LRU_BLOCK == 0
    nblk = 2 if nt > 1 else 1
    x_in, x_out, nsteps = _stream_specs(tt, bt, d, nt, batch_major_in)
    kern = functools.partial(_lru_kernel, tt=tt, bt=bt, nt=nt, batch_major_in=batch_major_in)
    row = lambda a: a.reshape(1, -1)
    return pl.pallas_call(
        kern,
        grid=(nb, nsteps),
        in_specs=[
            x_in,
            _const_spec((1, d)),
            _const_spec((d // 2, 2 * c)),
            _const_spec((LRU_CONV_W, c)),
            _const_spec((1, c)),
            _const_spec(w_gate.shape),
            _const_spec((1, c)),
            _const_spec((1, c)),
            _const_spec((1, c)),
            _const_spec((c // 2, d)),
            pl.BlockSpec((LRU_CONV_W - 1, bt, c), lambda ib, it: (0, ib, 0)),
            pl.BlockSpec((bt, c), lambda ib, it: (ib, 0)),
        ],
        out_specs=[
            x_out,
            pl.BlockSpec((LRU_CONV_W - 1, bt, c), lambda ib, it: (0, ib, 0)),
            pl.BlockSpec((bt, c), lambda ib, it: (ib, 0)),
        ],
        out_shape=[
            jax.ShapeDtypeStruct((t, b, d), F32),
            jax.ShapeDtypeStruct((LRU_CONV_W - 1, b, c), F32),
            jax.ShapeDtypeStruct((b, c), F32),
        ],
        scratch_shapes=[
            pltpu.VMEM((p + nblk * r, c), F32),
            pltpu.VMEM((nblk * r, c), F32),
            pltpu.VMEM((nblk * r, d), F32),
            pltpu.VMEM((r, d), BF16),
            pltpu.VMEM((r, c), BF16),
            pltpu.VMEM((r, d), F32),
            pltpu.VMEM((bt, c), F32),
        ],
        compiler_params=_compiler_params(),
        name="lru_layer",
    )(x, row(g), w_in, conv_w, row(conv_b), w_gate, row(b_a), row(b_i), row(lam), w_out, conv0, h0)


def _ccm_layer(x, conv0, g, w_in, b_in, dw_w, dw_b, ln_g, ln_b, w_out, b_out, final_g, *,
               tt, bt, final_norm, batch_major_out=False):
    t, b, d = x.shape
    c = dw_w.shape[-1]
    assert w_in.shape == (d // 2, 3 * c) and w_out.shape == (c // 2, d)
    nt = t // tt
    nb = b // bt
    r = tt * bt
    p = (CCM_CONV_W - 1) * bt
    go = min(tt, CONV_GROUP_STEPS)
    assert t % tt == 0 and b % bt == 0 and bt % SUBLANES == 0
    assert tt % go == 0 and go % 2 ** CONV_FIR_LEVELS == 0 and r % NORM_ROW_TILE == 0 and c % MXU_COLS == 0
    assert nt == 1 or r >= p
    assert r % min(r, max(r // 4, MIN_OUT_ROWS)) == 0
    kern = functools.partial(_ccm_kernel, tt=tt, bt=bt, nt=nt, final_norm=final_norm,
                             batch_major_out=batch_major_out)
    if batch_major_out:
        xo_spec = pl.BlockSpec((bt, tt, d), lambda ib, it: (ib, it, 0))
        xo_shape = jax.ShapeDtypeStruct((b, t, d), F32)
    else:
        xo_spec = pl.BlockSpec((tt, bt, d), lambda ib, it: (it, ib, 0))
        xo_shape = jax.ShapeDtypeStruct((t, b, d), F32)
    row = lambda a: a.reshape(1, -1)
    x_spec = pl.BlockSpec((tt, bt, d), lambda ib, it: (it, ib, 0))
    return pl.pallas_call(
        kern,
        grid=(nb, nt),
        in_specs=[
            x_spec,
            _const_spec((1, d)),
            _const_spec((d // 2, 3 * c)),
            _const_spec((1, 3 * c)),
            _const_spec((CCM_CONV_W, c)),
            _const_spec((1, c)),
            _const_spec((1, c)),
            _const_spec((1, c)),
            _const_spec((c // 2, d)),
            _const_spec((1, d)),
            _const_spec((1, d)),
            pl.BlockSpec((CCM_CONV_W - 1, bt, c), lambda ib, it: (0, ib, 0)),
        ],
        out_specs=[
            xo_spec,
            pl.BlockSpec((CCM_CONV_W - 1, bt, c), lambda ib, it: (0, ib, 0)),
        ],
        out_shape=[
            xo_shape,
            jax.ShapeDtypeStruct((CCM_CONV_W - 1, b, c), F32),
        ],
        scratch_shapes=[
            pltpu.VMEM((p + r + CONV_FIR_PAD * bt, c), F32),
            pltpu.VMEM((r, c), F32),
            pltpu.VMEM((r, d), BF16),
            pltpu.VMEM((r, c), F32),
            pltpu.VMEM((r, c), BF16),
            pltpu.VMEM((CCM_CONV_W, SUBLANES, c), F32),
        ],
        compiler_params=_compiler_params(),
        name="ccm_layer",
    )(x, row(g), w_in, row(b_in), dw_w, row(dw_b), row(ln_g), row(ln_b), w_out, row(b_out),
      row(final_g), conv0)


def kernel(x_prompt, x_sample, state_lru_conv, state_lru_h, state_ccm_conv, norm_g, final_norm_g,
           lru_w_in, lru_conv_w, lru_conv_b, lru_w_a, lru_b_a, lru_w_i, lru_b_i, lru_lam, lru_w_out,
           ccm_w_in, ccm_b_in, ccm_dw_w, ccm_dw_b, ccm_ln_g, ccm_ln_b, ccm_w_out, ccm_b_out):
    depth = norm_g.shape[0]
    bp = x_prompt.shape[0]
    c = lru_w_out.shape[1]
    assert depth % 2 == 0
    tm = lambda a: jnp.swapaxes(a, -3, -2)

    xp = x_prompt
    xs = tm(x_sample)
    s_lru_conv = tm(state_lru_conv)
    s_ccm_conv = tm(state_ccm_conv)
    z_lru_conv = jnp.zeros((LRU_CONV_W - 1, bp, c), F32)
    z_lru_h = jnp.zeros((bp, c), F32)
    z_ccm_conv = jnp.zeros((CCM_CONV_W - 1, bp, c), F32)

    lru_w_in_b = _pack_rows(lru_w_in)
    lru_w_out_b = _pack_rows(lru_w_out)
    lru_w_gate_b = _pack_rows(jnp.concatenate([lru_w_a, lru_w_i], axis=-1))
    ccm_w_in_b = _pack_rows(ccm_w_in)
    ccm_w_out_b = _pack_rows(ccm_w_out)

    lru_conv_p, lru_h_p, ccm_conv_p = [], [], []
    lru_conv_s, lru_h_s, ccm_conv_s = [], [], []
    for l in range(depth):
        j = l // 2
        if l % 2 == 0:
            lw = (norm_g[l], lru_w_in_b[j], lru_conv_w[j], lru_conv_b[j], lru_w_gate_b[j],
                  lru_b_a[j], lru_b_i[j], lru_lam[j], lru_w_out_b[j])
            xp, cb, hl = _lru_layer(xp, z_lru_conv, z_lru_h, *lw, tt=LRU_PROMPT_STEPS, bt=bp,
                                    batch_major_in=l == 0)
            lru_conv_p.append(cb)
            lru_h_p.append(hl)
            xs, cb, hl = _lru_layer(xs, s_lru_conv[j], state_lru_h[j], *lw, tt=xs.shape[0], bt=xs.shape[1])
            lru_conv_s.append(cb)
            lru_h_s.append(hl)
        else:
            last = l == depth - 1
            cw = (norm_g[l], ccm_w_in_b[j], ccm_b_in[j], ccm_dw_w[j], ccm_dw_b[j], ccm_ln_g[j],
                  ccm_ln_b[j], ccm_w_out_b[j], ccm_b_out[j], final_norm_g)
            xp, cb = _ccm_layer(xp, z_ccm_conv, *cw, tt=CCM_PROMPT_STEPS, bt=bp, final_norm=last,
                                batch_major_out=last)
            ccm_conv_p.append(cb)
            xs, cb = _ccm_layer(xs, s_ccm_conv[j], *cw, tt=xs.shape[0], bt=CCM_DECODE_BATCH_TILE,
                                final_norm=last)
            ccm_conv_s.append(cb)
    return (xp, tm(xs),
            tm(jnp.stack(lru_conv_p)), jnp.stack(lru_h_p), tm(jnp.stack(ccm_conv_p)),
            tm(jnp.stack(lru_conv_s)), jnp.stack(lru_h_s), tm(jnp.stack(ccm_conv_s)))
```
